```python
import math
import jax
import jax.numpy as jnp
from jax import lax
import numpy as np

D_MODEL = 2048
BATCH = 2
SEQ = 4096
DEPTH = 2
DEC_BATCH = 32
DEC_SEQ = 4
PAST_LEN = 8192
PAGE_SIZE = 128

N_MIXERS = 2
N_HEADS = 16
HEAD_DIM = D_MODEL // N_HEADS
MOBA_BLOCK = 256
MOBA_TOP_K = 3
Q_CHUNK = 32
ROPE_THETA = 10000.0
HGRN_EXPAND = 128
HGRN_HEADS = D_MODEL // HGRN_EXPAND
HGRN_DK = HGRN_EXPAND
HGRN_DV = D_MODEL // HGRN_HEADS
HGRN_CHUNK = 64
D_FF = ((8 * D_MODEL + 3 * 256 - 1) // (3 * 256)) * 256
N_ATTN_LAYERS = (DEPTH + 1) // 2
N_HGRN_LAYERS = DEPTH // 2
ALPHA = (2.0 * DEPTH) ** 0.25
BETA = (8.0 * DEPTH) ** -0.25
LN_EPS = 1e-5
RMS_EPS = 1e-6

kernel_name = 'moba_hgrn2_deepnorm_decode_step'

F32 = jnp.float32


def layer_norm(x, g, b):
    xf = x.astype(F32)
    mu = jnp.mean(xf, -1, keepdims=True)
    var = jnp.mean(jnp.square(xf - mu), -1, keepdims=True)
    return ((xf - mu) * lax.rsqrt(var + LN_EPS) * g.astype(F32) + b.astype(F32)).astype(x.dtype)


def rope(x, pos):
    half = HEAD_DIM // 2
    inv = ROPE_THETA ** (-jnp.arange(half, dtype=F32) * (2.0 / HEAD_DIM))
    ang = pos.astype(F32)[:, None] * inv[None, :]
    cos = jnp.cos(ang)[None, :, None, :]
    sin = jnp.sin(ang)[None, :, None, :]
    xf = x.astype(F32)
    x1, x2 = xf[..., :half], xf[..., half:]
    return jnp.concatenate([x1 * cos - x2 * sin, x2 * cos + x1 * sin], -1).astype(x.dtype)


def swiglu_ffn(x, w_in, w_out):
    gate, up = jnp.split(x @ w_in, 2, axis=-1)
    return (jax.nn.silu(gate) * up) @ w_out


def attn_project(x, w_qkv, pos):
    B, L, _ = x.shape
    q, k, v = jnp.split(x @ w_qkv, 3, axis=-1)
    shp = (B, L, N_HEADS, HEAD_DIM)
    return rope(q.reshape(shp), pos), rope(k.reshape(shp), pos), v.reshape(shp)


def moba_attend(q, q_pos, k_means, fetch_k, fetch_v):
    B, Lq = q.shape[0], q.shape[1]
    nb = k_means.shape[1]
    q_blk = q_pos // MOBA_BLOCK
    gate = jnp.einsum('blhd,bnhd->blhn', q.astype(F32), k_means)
    fully_past = jnp.arange(nb)[None, :] < q_blk[:, None]
    gate = jnp.where(fully_past[None, :, None, :], gate, -jnp.inf)
    _, sel = lax.top_k(gate, MOBA_TOP_K)
    qb = q_blk[None, :, None, None]
    sel_valid = sel < qb
    own = jnp.broadcast_to(qb, (B, Lq, N_HEADS, 1)).astype(sel.dtype)
    blocks = jnp.concatenate([sel, own], -1)
    blk_valid = jnp.concatenate([sel_valid, jnp.ones(own.shape, bool)], -1)
    pos = blocks[..., None] * MOBA_BLOCK + jnp.arange(MOBA_BLOCK, dtype=jnp.int32)
    valid = blk_valid[..., None] & (pos <= q_pos[None, :, None, None, None])
    nk = (MOBA_TOP_K + 1) * MOBA_BLOCK
    pos = pos.reshape(B, Lq, N_HEADS, nk)
    valid = valid.reshape(B, Lq, N_HEADS, nk)
    kk = fetch_k(pos)
    logits = jnp.einsum('blhd,blhnd->blhn', q, kk).astype(F32) * (HEAD_DIM ** -0.5)
    p = jax.nn.softmax(jnp.where(valid, logits, -jnp.inf), axis=-1)
    vv = fetch_v(pos)
    return jnp.einsum('blhn,blhnd->blhd', p.astype(vv.dtype), vv)


def moba_prompt(x, w_qkv, w_o):
    B, L, _ = x.shape
    pos = jnp.arange(L, dtype=jnp.int32)
    q, k, v = attn_project(x, w_qkv, pos)
    nb = max(-(-L // MOBA_BLOCK), MOBA_TOP_K)
    k_pad = jnp.pad(k, ((0, 0), (0, nb * MOBA_BLOCK - L), (0, 0), (0, 0)))
    k_means = k_pad.reshape(B, nb, MOBA_BLOCK, N_HEADS, HEAD_DIM).sum(2, dtype=F32) / MOBA_BLOCK
    b_idx = jnp.arange(B)[:, None, None, None]
    h_idx = jnp.arange(N_HEADS)[None, None, :, None]

    def fetch(rows):
        return lambda p: rows[b_idx, jnp.clip(p, 0, L - 1), h_idx]

    n_chunks = L // Q_CHUNK
    qc = q.reshape(B, n_chunks, Q_CHUNK, N_HEADS, HEAD_DIM).transpose(1, 0, 2, 3, 4)
    pc = pos.reshape(n_chunks, Q_CHUNK)
    out = lax.map(lambda a: moba_attend(a[0], a[1], k_means, fetch(k), fetch(v)), (qc, pc))
    out = out.transpose(1, 0, 2, 3, 4).reshape(B, L, D_MODEL)
    return out @ w_o, k, v


def moba_sample(x, cache_k, cache_v, layer, page_table, w_qkv, w_o):
    B, L, _ = x.shape
    pos = PAST_LEN + jnp.arange(L, dtype=jnp.int32)
    q, k, v = attn_project(x, w_qkv, pos)
    n_pages = PAST_LEN // PAGE_SIZE
    ppb = MOBA_BLOCK // PAGE_SIZE
    nb = max(-(-(PAST_LEN + L) // MOBA_BLOCK), MOBA_TOP_K)
    page_sums = cache_k[layer].sum(1, dtype=F32)[page_table]
    page_sums = jnp.pad(page_sums, ((0, 0), (0, nb * ppb - n_pages), (0, 0), (0, 0)))
    blk_sums = page_sums.reshape(B, nb, ppb, N_HEADS, HEAD_DIM).sum(2)
    blk_sums = blk_sums.at[:, pos // MOBA_BLOCK].add(k.astype(F32))
    k_means = blk_sums / MOBA_BLOCK
    b_idx = jnp.arange(B)[:, None, None, None]
    h_idx = jnp.arange(N_HEADS)[None, None, :, None]

    def fetch(pool, new):
        def f(p):
            lp = jnp.clip(p, 0, PAST_LEN - 1)
            page = page_table[b_idx, lp // PAGE_SIZE]
            past_rows = pool[layer, page, lp % PAGE_SIZE, h_idx]
            new_rows = new[b_idx, jnp.clip(p - PAST_LEN, 0, L - 1), h_idx]
            return jnp.where((p < PAST_LEN)[..., None], past_rows, new_rows)
        return f

    out = moba_attend(q, pos, k_means, fetch(cache_k, k), fetch(cache_v, v))
    return out.reshape(B, L, D_MODEL) @ w_o, k, v


def hgrn2_recurrence(q, k, log_f, v, s0):
    B, L, H, DK = q.shape
    DV = v.shape[-1]
    C = HGRN_CHUNK if L % HGRN_CHUNK == 0 else L
    n = L // C

    def chunks(a):
        return a.astype(F32).reshape(B, n, C, H, a.shape[-1]).transpose(1, 0, 3, 2, 4)

    causal = jnp.tril(jnp.ones((C, C), dtype=bool))[None, None, :, :, None]

    def step(S, inp):
        qc, kc, gc, vc = inp
        b = jnp.cumsum(gc, axis=2)
        o_inter = jnp.einsum('bhtk,bhkv->bhtv', qc * jnp.exp(b), S)
        rel = jnp.exp(jnp.where(causal, b[:, :, :, None, :] - b[:, :, None, :, :], -jnp.inf))
        attn = jnp.einsum('bhtk,bhsk,bhtsk->bhts', qc, kc, rel)
        o = o_inter + jnp.einsum('bhts,bhsv->bhtv', attn, vc)
        b_last = b[:, :, -1, :]
        S = jnp.exp(b_last)[..., None] * S + jnp.einsum('bhsk,bhsv->bhkv', kc * jnp.exp(b_last[:, :, None, :] - b), vc)
        return S, o

    S, o = lax.scan(step, s0.astype(F32), (chunks(q), chunks(k), chunks(log_f), chunks(v)))
    return o.transpose(1, 0, 3, 2, 4).reshape(B, L, H, DV), S


def hgrn2_mixer(x, s0, w_in, norm_g, w_o, lb):
    B, L, _ = x.shape
    q, fz, i, g = jnp.split(x @ w_in, 4, axis=-1)
    f = lb + (1.0 - lb) * jax.nn.sigmoid(fz.astype(F32))
    shp = (B, L, HGRN_HEADS, HGRN_DK)
    o, s = hgrn2_recurrence(jax.nn.silu(q).reshape(shp), (1.0 - f).reshape(shp), jnp.log(f).reshape(shp),
                            i.reshape(B, L, HGRN_HEADS, HGRN_DV), s0)
    o = o * lax.rsqrt(jnp.mean(o * o, -1, keepdims=True) + RMS_EPS) * norm_g.astype(F32)
    o = (o.reshape(B, L, D_MODEL) * jax.nn.silu(g.astype(F32))).astype(x.dtype)
    return o @ w_o, s.astype(s0.dtype)


def setup_inputs(seed: int = 0) -> dict:
    key = jax.random.key(seed)
    ks = jax.random.split(key, 24)
    n_pages = PAST_LEN // PAGE_SIZE
    n_phys = (DEC_BATCH * n_pages * 5) // 4
    s = D_MODEL ** -0.5
    nrm = jax.random.normal
    page_table = jax.random.permutation(ks[5], n_phys)[:DEC_BATCH * n_pages].reshape(DEC_BATCH, n_pages).astype(jnp.int32)
    return {
        'x_prompt': nrm(ks[0], (BATCH, SEQ, D_MODEL), F32),
        'x_sample': nrm(ks[1], (DEC_BATCH, DEC_SEQ, D_MODEL), F32),
        'cache_k': nrm(ks[2], (N_ATTN_LAYERS, n_phys, PAGE_SIZE, N_HEADS, HEAD_DIM), F32),
        'cache_v': nrm(ks[3], (N_ATTN_LAYERS, n_phys, PAGE_SIZE, N_HEADS, HEAD_DIM), F32),
        'state_hgrn': 0.5 * nrm(ks[4], (N_HGRN_LAYERS, DEC_BATCH, HGRN_HEADS, HGRN_DK, HGRN_DV), F32),
        'page_table': page_table,
        'attn_w_qkv': s * nrm(ks[6], (N_ATTN_LAYERS, D_MODEL, 3 * D_MODEL), F32),
        'attn_w_o': (s * BETA) * nrm(ks[7], (N_ATTN_LAYERS, D_MODEL, D_MODEL), F32),
        'hgrn_w_in': s * nrm(ks[8], (N_HGRN_LAYERS, D_MODEL, 4 * D_MODEL), F32),
        'hgrn_lb_logits': 0.1 * nrm(ks[9], (DEPTH, D_MODEL), F32),
        'hgrn_norm_g': 1.0 + 0.02 * nrm(ks[10], (N_HGRN_LAYERS, HGRN_DV), F32),
        'hgrn_w_o': (s * BETA) * nrm(ks[11], (N_HGRN_LAYERS, D_MODEL, D_MODEL), F32),
        'ffn_w_in': s * nrm(ks[12], (DEPTH, D_MODEL, 2 * D_FF), F32),
        'ffn_w_out': (D_FF ** -0.5 * BETA) * nrm(ks[13], (DEPTH, D_FF, D_MODEL), F32),
        'ln_mix_g': 1.0 + 0.02 * nrm(ks[14], (DEPTH, D_MODEL), F32),
        'ln_mix_b': 0.02 * nrm(ks[15], (DEPTH, D_MODEL), F32),
        'ln_ffn_g': 1.0 + 0.02 * nrm(ks[16], (DEPTH, D_MODEL), F32),
        'ln_ffn_b': 0.02 * nrm(ks[17], (DEPTH, D_MODEL), F32),
    }


def reference(x_prompt, x_sample, cache_k, cache_v, state_hgrn, page_table,
              attn_w_qkv, attn_w_o, hgrn_w_in, hgrn_lb_logits, hgrn_norm_g, hgrn_w_o,
              ffn_w_in, ffn_w_out, ln_mix_g, ln_mix_b, ln_ffn_g, ln_ffn_b):
    lb_all = jnp.cumsum(jax.nn.softmax(hgrn_lb_logits.astype(F32), axis=0), axis=0)
    lb_all = lb_all - lb_all[0]
    yp, ys = x_prompt, x_sample
    pk, pv, ps, sk, sv, ss = [], [], [], [], [], []
    for i in range(DEPTH):
        if i % N_MIXERS == 0:
            a = i // N_MIXERS
            mp, kp, vp = moba_prompt(yp, attn_w_qkv[a], attn_w_o[a])
            ms, kn, vn = moba_sample(ys, cache_k, cache_v, a, page_table, attn_w_qkv[a], attn_w_o[a])
            pk.append(kp)
            pv.append(vp)
            sk.append(kn)
            sv.append(vn)
        else:
            r = i // N_MIXERS
            s0p = jnp.zeros((yp.shape[0], HGRN_HEADS, HGRN_DK, HGRN_DV), state_hgrn.dtype)
            mp, sp = hgrn2_mixer(yp, s0p, hgrn_w_in[r], hgrn_norm_g[r], hgrn_w_o[r], lb_all[i])
            ms, sn = hgrn2_mixer(ys, state_hgrn[r], hgrn_w_in[r], hgrn_norm_g[r], hgrn_w_o[r], lb_all[i])
            ps.append(sp)
            ss.append(sn)
        yp = layer_norm(ALPHA * yp + mp, ln_mix_g[i], ln_mix_b[i])
        ys = layer_norm(ALPHA * ys + ms, ln_mix_g[i], ln_mix_b[i])
        yp = layer_norm(ALPHA * yp + swiglu_ffn(yp, ffn_w_in[i], ffn_w_out[i]), ln_ffn_g[i], ln_ffn_b[i])
        ys = layer_norm(ALPHA * ys + swiglu_ffn(ys, ffn_w_in[i], ffn_w_out[i]), ln_ffn_g[i], ln_ffn_b[i])
    return (yp, ys, jnp.stack(pk), jnp.stack(pv), jnp.stack(ps), jnp.stack(sk), jnp.stack(sv), jnp.stack(ss))
```

```python
import functools
import math

import numpy as np
import jax
import jax.numpy as jnp
from jax import lax
from jax.experimental import pallas as pl
from jax.experimental.pallas import tpu as pltpu

F32 = jnp.float32
BF16 = jnp.bfloat16
NEG_INF = float("-inf")

N_HEADS = 16
MOBA_BLOCK = 256
MOBA_TOP_K = 3
ROPE_THETA = 10000.0
HGRN_EXPAND = 128
HGRN_CHUNK = 64
LN_EPS = 1e-5
RMS_EPS = 1e-6

LANES = 128
SUBLANES = 8
BF16_SUBLANES = 16
VMEM_LIMIT_BYTES = 56 * 1024 * 1024

NT_DIMS = (((1,), (1,)), ((), ()))


def _params(*semantics):
    return pltpu.CompilerParams(dimension_semantics=semantics, vmem_limit_bytes=VMEM_LIMIT_BYTES)


def _row_tile(m, target):
    if m <= target:
        return m
    t = target
    while m % t:
        t //= 2
    return t


def _dot(a, b):
    return jnp.dot(a, b, preferred_element_type=F32)


def _dot_nt(a, b, precision=None):
    return lax.dot_general(a, b, NT_DIMS, precision=precision, preferred_element_type=F32)


def _sigmoid(x):
    return 1.0 / (1.0 + jnp.exp(-x))


def _layer_norm(z, g, b):
    mu = jnp.mean(z, axis=-1, keepdims=True)
    zc = z - mu
    var = jnp.mean(zc * zc, axis=-1, keepdims=True)
    return zc * lax.rsqrt(var + LN_EPS) * g + b


def _top_k_columns(gate, colf, n_cols):
    picks = []
    for _ in range(MOBA_TOP_K):
        m = jnp.max(gate, axis=1, keepdims=True)
        idx = jnp.min(jnp.where(gate == m, colf, float(n_cols)), axis=1, keepdims=True)
        picks.append(jnp.where(m > NEG_INF, idx, -1.0))
        gate = jnp.where(colf == idx, NEG_INF, gate)
    return picks


def _qkv_rope_kernel(x_ref, wq_ref, wk_ref, wv_ref, cos_ref, sin_ref, q_ref, k_ref, v_ref, *, head_dim):
    x = x_ref[...]
    cos = cos_ref[...]
    sin = sin_ref[...]
    q = _dot(x, wq_ref[...])
    k = _dot(x, wk_ref[...])
    v_ref[...] = _dot(x, wv_ref[...])
    for h in range(q.shape[1] // head_dim):
        sl = slice(h * head_dim, (h + 1) * head_dim)
        qh = q[:, sl]
        kh = k[:, sl]
        q_ref[:, sl] = qh * cos + pltpu.roll(qh, head_dim // 2, axis=1) * sin
        k_ref[:, sl] = kh * cos + pltpu.roll(kh, head_dim // 2, axis=1) * sin


def _rope_tables(pos, head_dim):
    half = head_dim // 2
    inv = ROPE_THETA ** (-jnp.arange(half, dtype=F32) * (2.0 / head_dim))
    ang = pos.astype(F32)[:, None] * inv[None, :]
    cos, sin = jnp.cos(ang), jnp.sin(ang)
    return jnp.concatenate([cos, cos], -1), jnp.concatenate([-sin, sin], -1)


def _qkv_rope(x_bf, w_qkv_bf, cos, sin):
    m, d = x_bf.shape
    head_dim = d // N_HEADS
    tm = _row_tile(m, 1024)
    tn = 512
    nj = d // tn
    out = jax.ShapeDtypeStruct((m, d), F32)
    o_spec = pl.BlockSpec((tm, tn), lambda j, i: (i, j))
    return pl.pallas_call(
        functools.partial(_qkv_rope_kernel, head_dim=head_dim),
        out_shape=(out, out, out),
        grid=(nj, m // tm),
        in_specs=[
            pl.BlockSpec((tm, d), lambda j, i: (i, 0)),
            pl.BlockSpec((d, tn), lambda j, i: (0, j)),
            pl.BlockSpec((d, tn), lambda j, i: (0, nj + j)),
            pl.BlockSpec((d, tn), lambda j, i: (0, 2 * nj + j)),
            pl.BlockSpec((tm, head_dim), lambda j, i: (i, 0)),
            pl.BlockSpec((tm, head_dim), lambda j, i: (i, 0)),
        ],
        out_specs=(o_spec, o_spec, o_spec),
        compiler_params=_params("parallel", "parallel"),
        name="qkv_rope",
    )(x_bf, w_qkv_bf, w_qkv_bf, w_qkv_bf, cos, sin)


def _moba_prompt_kernel(q_ref, k_ref, v_ref, o_ref, kb_ref, vb_ref, km_ref, *, n_blocks, scale):
    qi = pl.program_id(2)
    blk = MOBA_BLOCK

    @pl.when(qi == 0)
    def _():
        for n in range(n_blocks):
            rows = slice(n * blk, (n + 1) * blk)
            kf = k_ref[rows, :]
            km_ref[n:n + 1, :] = jnp.sum(kf, axis=0, keepdims=True) * (1.0 / blk)
            kb_ref[rows, :] = kf.astype(BF16)
            vb_ref[rows, :] = v_ref[rows, :].astype(BF16)

    qf = q_ref[...]
    gate = _dot_nt(qf, km_ref[...], precision=lax.Precision.HIGHEST)
    col = lax.broadcasted_iota(jnp.int32, gate.shape, 1)
    gate = jnp.where(col < qi, gate, NEG_INF)
    s1, s2, s3 = _top_k_columns(gate, col.astype(F32), n_blocks)

    qb = qf.astype(BF16)
    own = pl.ds(pl.multiple_of(qi * blk, blk), blk)
    s = _dot_nt(qb, kb_ref[own, :]) * scale
    r_id = lax.broadcasted_iota(jnp.int32, s.shape, 0)
    c_id = lax.broadcasted_iota(jnp.int32, s.shape, 1)
    s = jnp.where(c_id <= r_id, s, NEG_INF)
    m0 = jnp.max(s, axis=1, keepdims=True)
    p = jnp.exp(s - m0)
    l0 = jnp.sum(p, axis=1, keepdims=True)
    acc0 = _dot(p.astype(BF16), vb_ref[own, :])

    def body(n, carry):
        m, l, acc = carry
        rows = pl.ds(pl.multiple_of(n * blk, blk), blk)
        nf = n.astype(F32)
        picked = (s1 == nf) | (s2 == nf) | (s3 == nf)
        sn = jnp.where(picked, _dot_nt(qb, kb_ref[rows, :]) * scale, NEG_INF)
        m_new = jnp.maximum(m, jnp.max(sn, axis=1, keepdims=True))
        alpha = jnp.exp(m - m_new)
        pn = jnp.exp(sn - m_new)
        l = alpha * l + jnp.sum(pn, axis=1, keepdims=True)
        acc = alpha * acc + _dot(pn.astype(BF16), vb_ref[rows, :])
        return m_new, l, acc

    _, l, acc = lax.fori_loop(0, qi, body, (m0, l0, acc0))
    o_ref[...] = (acc / l).astype(o_ref.dtype)


def _moba_prompt_attention(q, k, v, batch):
    m, d = q.shape
    seq = m // batch
    head_dim = d // N_HEADS
    assert seq % MOBA_BLOCK == 0 and seq // MOBA_BLOCK >= MOBA_TOP_K
    n_blocks = seq // MOBA_BLOCK
    kv_spec = pl.BlockSpec((seq, head_dim), lambda b, h, i: (b, h))
    qo_spec = pl.BlockSpec((MOBA_BLOCK, head_dim), lambda b, h, i: (b * n_blocks + i, h))
    return pl.pallas_call(
        functools.partial(_moba_prompt_kernel, n_blocks=n_blocks, scale=head_dim ** -0.5),
        out_shape=jax.ShapeDtypeStruct((m, d), BF16),
        grid=(batch, N_HEADS, n_blocks),
        in_specs=[qo_spec, kv_spec, kv_spec],
        out_specs=qo_spec,
        scratch_shapes=[
            pltpu.VMEM((seq, head_dim), BF16),
            pltpu.VMEM((seq, head_dim), BF16),
            pltpu.VMEM((n_blocks, head_dim), F32),
        ],
        compiler_params=_params("parallel", "parallel", "arbitrary"),
        name="moba_prompt_attention",
    )(q, k, v)


def _sample_stream_kernel(pt_ref, q_ref, k0_ref, k1_ref, v0_ref, v1_ref, km_ref, ml_ref, acc_ref, *, scale):
    del pt_ref
    n_heads = q_ref.shape[0]
    inv_blk = 1.0 / (2 * k0_ref.shape[0])
    lane = lax.broadcasted_iota(jnp.int32, ml_ref.shape[1:], 1)
    for h in range(n_heads):
        k0 = k0_ref[:, h, :]
        k1 = k1_ref[:, h, :]
        km_ref[h:h + 1, :] = (jnp.sum(k0, axis=0, keepdims=True) + jnp.sum(k1, axis=0, keepdims=True)) * inv_blk
        qb = q_ref[h].astype(BF16)
        s0 = _dot_nt(qb, k0.astype(BF16)) * scale
        s1 = _dot_nt(qb, k1.astype(BF16)) * scale
        m = jnp.maximum(jnp.max(s0, axis=1, keepdims=True), jnp.max(s1, axis=1, keepdims=True))
        p0 = jnp.exp(s0 - m)
        p1 = jnp.exp(s1 - m)
        l = jnp.sum(p0, axis=1, keepdims=True) + jnp.sum(p1, axis=1, keepdims=True)
        acc_ref[h] = (_dot(p0.astype(BF16), v0_ref[:, h, :].astype(BF16))
                      + _dot(p1.astype(BF16), v1_ref[:, h, :].astype(BF16)))
        ml_ref[h] = jnp.where(lane == 0, m, jnp.where(lane == 1, l, 0.0))


def _sample_stream(q_hl, cache_k, cache_v, layer, page_table):
    n_seq, n_heads, l8, head_dim = q_hl.shape
    page = cache_k.shape[2]
    assert MOBA_BLOCK == 2 * page
    n_blk = page_table.shape[1] // 2

    def page_spec(j):
        return pl.BlockSpec((None, None, page, n_heads, head_dim),
                            lambda b, n, pt: (layer, pt[b, 2 * n + j], 0, 0, 0))

    part = jax.ShapeDtypeStruct((n_seq, n_blk, n_heads, l8, head_dim), F32)
    part_spec = pl.BlockSpec((None, None, n_heads, l8, head_dim), lambda b, n, pt: (b, n, 0, 0, 0))
    return pl.pallas_call(
        functools.partial(_sample_stream_kernel, scale=head_dim ** -0.5),
        out_shape=(jax.ShapeDtypeStruct((n_seq, n_blk, n_heads, head_dim), F32), part, part),
        grid_spec=pltpu.PrefetchScalarGridSpec(
            num_scalar_prefetch=1,
            grid=(n_seq, n_blk),
            in_specs=[
                pl.BlockSpec((None, n_heads, l8, head_dim), lambda b, n, pt: (b, 0, 0, 0)),
                page_spec(0), page_spec(1), page_spec(0), page_spec(1),
            ],
            out_specs=(
                pl.BlockSpec((None, None, n_heads, head_dim), lambda b, n, pt: (b, n, 0, 0)),
                part_spec, part_spec,
            ),
        ),
        compiler_params=_params("parallel", "arbitrary"),
        name="sample_cache_stream",
    )(page_table, q_hl, cache_k, cache_k, cache_v, cache_v)


def _sample_combine_kernel(q_ref, km_ref, kn_ref, vn_ref, ml_ref, acc_ref, o_ref, kmx_ref,
                           *, n_heads, l8, n_new, q_blk, scale):
    n_blk = ml_ref.shape[0]
    n_past = n_blk * n_heads
    qf = q_ref[...]
    kn = kn_ref[...]
    rows = qf.shape[0]

    kmx_ref[0:n_past, :] = km_ref[...]
    own_sum = jnp.sum(kn.reshape(n_heads, l8, kn.shape[1]), axis=1)
    kmx_ref[n_past:n_past + n_heads, :] = own_sum * (1.0 / MOBA_BLOCK)
    gate = _dot_nt(qf, kmx_ref[...], precision=lax.Precision.HIGHEST)
    col = lax.broadcasted_iota(jnp.int32, gate.shape, 1)
    row = lax.broadcasted_iota(jnp.int32, gate.shape, 0)
    valid = ((col % n_heads) == (row // l8)) & ((col // n_heads) < q_blk)
    gate = jnp.where(valid, gate, NEG_INF)
    s1, s2, s3 = _top_k_columns(gate, col.astype(F32), gate.shape[1])
    head_of_row = (lax.broadcasted_iota(jnp.int32, (rows, 1), 0) // l8).astype(F32)

    def picked(n):
        tgt = n.astype(F32) * float(n_heads) + head_of_row
        return (s1 == tgt) | (s2 == tgt) | (s3 == tgt)

    so = _dot_nt(qf.astype(BF16), kn.astype(BF16)) * scale
    r2 = lax.broadcasted_iota(jnp.int32, so.shape, 0)
    c2 = lax.broadcasted_iota(jnp.int32, so.shape, 1)
    ok = ((c2 // l8) == (r2 // l8)) & ((c2 % l8) <= (r2 % l8)) & ((c2 % l8) < n_new)
    so = jnp.where(ok, so, NEG_INF)
    m_own = jnp.max(so, axis=1, keepdims=True)

    def max_body(n, m):
        return jnp.maximum(m, jnp.where(picked(n), ml_ref[n][:, 0:1], NEG_INF))

    m_all = lax.fori_loop(0, n_blk, max_body, m_own)
    p_own = jnp.exp(so - m_all)
    l_own = jnp.sum(p_own, axis=1, keepdims=True)
    acc_own = _dot(p_own.astype(BF16), vn_ref[...].astype(BF16))

    def sum_body(n, carry):
        l, acc = carry
        ml = ml_ref[n]
        w = jnp.where(picked(n), jnp.exp(ml[:, 0:1] - m_all), 0.0)
        return l + w * ml[:, 1:2], acc + w * acc_ref[n]

    l, acc = lax.fori_loop(0, n_blk, sum_body, (l_own, acc_own))
    o_ref[...] = acc / l


def _sample_combine(q_hl, km, k_hl, v_hl, ml, acc, n_new, q_blk):
    n_seq, n_heads, l8, head_dim = q_hl.shape
    n_blk = km.shape[1]
    rows = n_heads * l8
    flat = lambda a: a.reshape(n_seq, rows, head_dim)
    seq_spec = pl.BlockSpec((None, rows, head_dim), lambda b: (b, 0, 0))
    part_spec = pl.BlockSpec((None, n_blk, rows, head_dim), lambda b: (b, 0, 0, 0))
    out = pl.pallas_call(
        functools.partial(_sample_combine_kernel, n_heads=n_heads, l8=l8, n_new=n_new, q_blk=q_blk,
                          scale=head_dim ** -0.5),
        out_shape=jax.ShapeDtypeStruct((n_seq, rows, head_dim), F32),
        grid=(n_seq,),
        in_specs=[
            seq_spec,
            pl.BlockSpec((None, n_blk * n_heads, head_dim), lambda b: (b, 0, 0)),
            seq_spec, seq_spec, part_spec, part_spec,
        ],
        out_specs=seq_spec,
        scratch_shapes=[pltpu.VMEM(((n_blk + 1) * n_heads, head_dim), F32)],
        compiler_params=_params("parallel"),
        name="sample_select_combine",
    )(flat(q_hl), km.reshape(n_seq, n_blk * n_heads, head_dim), flat(k_hl), flat(v_hl),
      ml.reshape(n_seq, n_blk, rows, head_dim), acc.reshape(n_seq, n_blk, rows, head_dim))
    return out.reshape(n_seq, n_heads, l8, head_dim)


def _proj_ln_kernel(x_ref, w_ref, r_ref, g_ref, b_ref, y_ref, yb_ref, *, alpha):
    z = alpha * r_ref[...] + _dot(x_ref[...], w_ref[...])
    y = _layer_norm(z, g_ref[...], b_ref[...])
    y_ref[...] = y
    yb_ref[...] = y.astype(BF16)


def _proj_ln(x_bf, w_bf, resid, g, b, alpha):
    m, d_in = x_bf.shape
    d = w_bf.shape[1]
    tm = _row_tile(m, 256)
    row = lambda width: pl.BlockSpec((tm, width), lambda i: (i, 0))
    vec = pl.BlockSpec((1, d), lambda i: (0, 0))
    return pl.pallas_call(
        functools.partial(_proj_ln_kernel, alpha=alpha),
        out_shape=(jax.ShapeDtypeStruct((m, d), F32), jax.ShapeDtypeStruct((m, d), BF16)),
        grid=(m // tm,),
        in_specs=[row(d_in), pl.BlockSpec((d_in, d), lambda i: (0, 0)), row(d), vec, vec],
        out_specs=(row(d), row(d)),
        compiler_params=_params("parallel"),
        name="proj_residual_ln",
    )(x_bf, w_bf, resid, g.reshape(1, d), b.reshape(1, d))


def _ffn_ln_kernel(x_ref, wg_ref, wu_ref, wo_ref, r_ref, g_ref, b_ref, y_ref, yb_ref, acc_ref, *, alpha):
    f = pl.program_id(1)
    x = x_ref[...]
    gate = _dot(x, wg_ref[...])
    up = _dot(x, wu_ref[...])
    hidden = (gate * _sigmoid(gate) * up).astype(BF16)
    part = _dot(hidden, wo_ref[...])

    @pl.when(f == 0)
    def _():
        acc_ref[...] = part

    @pl.when(f > 0)
    def _():
        acc_ref[...] += part

    @pl.when(f == pl.num_programs(1) - 1)
    def _():
        y = _layer_norm(alpha * r_ref[...] + acc_ref[...], g_ref[...], b_ref[...])
        y_ref[...] = y
        yb_ref[...] = y.astype(BF16)


def _ffn_ln(x_bf, w_in_bf, w_out_bf, resid, g, b, alpha):
    m, d = x_bf.shape
    d_ff = w_out_bf.shape[0]
    tm = _row_tile(m, 512)
    tf = 512
    assert d_ff % tf == 0
    nf = d_ff // tf
    row = lambda: pl.BlockSpec((tm, d), lambda i, f: (i, 0))
    vec = pl.BlockSpec((1, d), lambda i, f: (0, 0))
    return pl.pallas_call(
        functools.partial(_ffn_ln_kernel, alpha=alpha),
        out_shape=(jax.ShapeDtypeStruct((m, d), F32), jax.ShapeDtypeStruct((m, d), BF16)),
        grid=(m // tm, nf),
        in_specs=[
            row(),
            pl.BlockSpec((d, tf), lambda i, f: (0, f)),
            pl.BlockSpec((d, tf), lambda i, f: (0, nf + f)),
            pl.BlockSpec((tf, d), lambda i, f: (f, 0)),
            row(), vec, vec,
        ],
        out_specs=(row(), row()),
        scratch_shapes=[pltpu.VMEM((tm, d), F32)],
        compiler_params=_params("parallel", "arbitrary"),
        name="swiglu_residual_ln",
    )(x_bf, w_in_bf, w_in_bf, w_out_bf, resid, g.reshape(1, d), b.reshape(1, d))


def _matmul_heads_kernel(x_ref, w_ref, o_ref):
    acc = _dot(x_ref[...], w_ref[...])
    width = o_ref.shape[2]
    for h in range(o_ref.shape[0]):
        o_ref[h] = acc[:, h * width:(h + 1) * width]


def _matmul_heads(x_bf, w_bf, width):
    m, d = x_bf.shape
    n_out = w_bf.shape[1]
    tm = _row_tile(m, 1024)
    tn = 512
    return pl.pallas_call(
        _matmul_heads_kernel,
        out_shape=jax.ShapeDtypeStruct((n_out // width, m, width), F32),
        grid=(n_out // tn, m // tm),
        in_specs=[pl.BlockSpec((tm, d), lambda j, i: (i, 0)), pl.BlockSpec((d, tn), lambda j, i: (0, j))],
        out_specs=pl.BlockSpec((tn // width, tm, width), lambda j, i: (j, i, 0)),
        compiler_params=_params("parallel", "parallel"),
        name="hgrn_in_proj",
    )(x_bf, w_bf)


def _hgrn_level_matrix(chunk):
    n_lev = int(math.log2(chunk))
    assert 2 ** n_lev == chunk
    t = np.arange(chunk)[:, None]
    s = np.arange(chunk)[None, :]
    mats = [s <= t]
    for lev in range(1, n_lev + 1):
        size = 2 ** lev
        mats.append(s < (t // size) * size + size // 2)
    return np.concatenate(mats, 0).astype(np.float32), n_lev


def _hgrn_kernel(q_ref, f_ref, i_ref, g_ref, lbl_ref, ng_ref, cm_ref, s0_ref, o_ref, s_ref, st_ref, oh_ref,
                 *, layer, n_valid, n_lev):
    c = pl.program_id(1)
    n_heads, chunk, dk = q_ref.shape

    @pl.when(c == 0)
    def _():
        for h in range(n_heads):
            st_ref[h] = s0_ref[h].T

    row = lax.broadcasted_iota(jnp.int32, (chunk, dk), 0)
    t_id = lax.broadcasted_iota(jnp.int32, (chunk, chunk), 0)
    s_id = lax.broadcasted_iota(jnp.int32, (chunk, chunk), 1)
    cm = cm_ref[...]
    ng = ng_ref[...]
    lrow = lax.broadcasted_iota(jnp.int32, lbl_ref.shape[1:], 0)

    def head(h, carry):
        logits = lbl_ref[h]
        e = jnp.exp(logits - jnp.max(logits, axis=0, keepdims=True))
        sm = e / jnp.sum(e, axis=0, keepdims=True)
        lb = jnp.sum(jnp.where((lrow >= 1) & (lrow <= layer), sm, 0.0), axis=0, keepdims=True)

        qraw = q_ref[h]
        q = qraw * _sigmoid(qraw)
        f = lb + (1.0 - lb) * _sigmoid(f_ref[h])
        k = 1.0 - f
        g = jnp.log(f)
        if n_valid < chunk:
            live = row < n_valid
            k = jnp.where(live, k, 0.0)
            g = jnp.where(live, g, 0.0)
        v = i_ref[h]
        vb = v.astype(BF16)

        g1 = g.astype(BF16)
        r1 = g - g1.astype(F32)
        g2 = r1.astype(BF16)
        g3 = (r1 - g2.astype(F32)).astype(BF16)
        b_all = _dot(cm, g1) + _dot(cm, g2) + _dot(cm, g3)
        b = b_all[0:chunk]

        qb = q.astype(BF16)
        a = jnp.where(t_id == s_id, _dot_nt(qb, k.astype(BF16)), 0.0)
        for lev in range(1, n_lev + 1):
            size = 2 ** lev
            half = size // 2
            b_mid = b_all[lev * chunk:(lev + 1) * chunk]
            decay = jnp.exp(-jnp.abs(b - b_mid))
            x = (jnp.where((row % size) >= half, q, k) * decay).astype(BF16)
            pair = ((t_id // size) == (s_id // size)) & ((t_id % size) >= half) & ((s_id % size) < half)
            a = a + jnp.where(pair, _dot_nt(x, x), 0.0)

        st = st_ref[h]
        o = _dot_nt((q * jnp.exp(b)).astype(BF16), st.astype(BF16)) + _dot(a.astype(BF16), vb)

        b_last = b[chunk - 1:chunk, :]
        kd = (k * jnp.exp(b_last - b)).astype(BF16)
        st_ref[h] = st * jnp.exp(b_last) + _dot(v.T.astype(BF16), kd)

        o = o * lax.rsqrt(jnp.mean(o * o, axis=-1, keepdims=True) + RMS_EPS) * ng
        graw = g_ref[h]
        oh_ref[h] = (o * (graw * _sigmoid(graw))).astype(oh_ref.dtype)
        return carry

    lax.fori_loop(0, n_heads, head, 0)

    width = oh_ref.shape[2]
    for h in range(n_heads):
        o_ref[:, h * width:(h + 1) * width] = oh_ref[h]

    @pl.when(c == pl.num_programs(1) - 1)
    def _():
        for h in range(n_heads):
            s_ref[h] = st_ref[h].T


def _hgrn_recurrence(y_heads, s0, lb_logits, norm_g, layer, n_valid, chunk):
    _, n_heads, batch, lp, dk = y_heads.shape
    dv = s0.shape[-1]
    depth = lb_logits.shape[0]
    cm_np, n_lev = _hgrn_level_matrix(chunk)
    cm = jnp.asarray(cm_np, dtype=BF16)
    lbl = lb_logits.reshape(depth, n_heads, dk).transpose(1, 0, 2)

    def quarter(qt):
        return pl.BlockSpec((None, n_heads, None, chunk, dk), lambda b, c: (qt, 0, b, c, 0))

    state_spec = pl.BlockSpec((None, n_heads, dk, dv), lambda b, c: (b, 0, 0, 0))
    return pl.pallas_call(
        functools.partial(_hgrn_kernel, layer=layer, n_valid=n_valid, n_lev=n_lev),
        out_shape=(jax.ShapeDtypeStruct((batch, lp, n_heads * dv), BF16),
                   jax.ShapeDtypeStruct((batch, n_heads, dk, dv), F32)),
        grid=(batch, lp // chunk),
        in_specs=[
            quarter(0), quarter(1), quarter(2), quarter(3),
            pl.BlockSpec((n_heads, depth, dk), lambda b, c: (0, 0, 0)),
            pl.BlockSpec((1, dv), lambda b, c: (0, 0)),
            pl.BlockSpec(cm.shape, lambda b, c: (0, 0)),
            state_spec,
        ],
        out_specs=(pl.BlockSpec((None, chunk, n_heads * dv), lambda b, c: (b, c, 0)), state_spec),
        scratch_shapes=[pltpu.VMEM((n_heads, dv, dk), F32), pltpu.VMEM((n_heads, chunk, dv), BF16)],
        compiler_params=_params("parallel", "arbitrary"),
        name="hgrn_recurrence",
    )(y_heads, y_heads, y_heads, y_heads, lbl, norm_g.reshape(1, dv), cm, s0)


def _heads_major(a, n_seq, n_new, l8):
    d = a.shape[-1]
    a = a.reshape(n_seq, n_new, N_HEADS, d // N_HEADS).transpose(0, 2, 1, 3)
    return jnp.pad(a, ((0, 0), (0, 0), (0, l8 - n_new), (0, 0)))


def _moba_layer(xp, xp_bf, xs, xs_bf, batch, n_seq, cache_k, cache_v, layer, page_table, w_qkv_bf, w_o_bf,
                ln_g, ln_b, alpha):
    d = xp.shape[-1]
    head_dim = d // N_HEADS
    seq = xp.shape[0] // batch
    n_new = xs.shape[0] // n_seq
    past = page_table.shape[1] * cache_k.shape[2]
    assert past % MOBA_BLOCK == 0 and n_new <= MOBA_BLOCK

    cos_p, sin_p = _rope_tables(jnp.arange(seq, dtype=jnp.int32), head_dim)
    qp, kp, vp = _qkv_rope(xp_bf, w_qkv_bf, jnp.tile(cos_p, (batch, 1)), jnp.tile(sin_p, (batch, 1)))
    op = _moba_prompt_attention(qp, kp, vp, batch)
    yp, yp_bf = _proj_ln(op, w_o_bf, xp, ln_g, ln_b, alpha)

    cos_s, sin_s = _rope_tables(past + jnp.arange(n_new, dtype=jnp.int32), head_dim)
    qs, ks, vs = _qkv_rope(xs_bf, w_qkv_bf, jnp.tile(cos_s, (n_seq, 1)), jnp.tile(sin_s, (n_seq, 1)))
    l8 = -(-n_new // SUBLANES) * SUBLANES
    q_hl = _heads_major(qs, n_seq, n_new, l8)
    km, ml, acc = _sample_stream(q_hl, cache_k, cache_v, layer, page_table)
    o_hl = _sample_combine(q_hl, km, _heads_major(ks, n_seq, n_new, l8), _heads_major(vs, n_seq, n_new, l8),
                           ml, acc, n_new, past // MOBA_BLOCK)
    os_ = o_hl[:, :, :n_new].transpose(0, 2, 1, 3).reshape(n_seq * n_new, d).astype(BF16)
    ys, ys_bf = _proj_ln(os_, w_o_bf, xs, ln_g, ln_b, alpha)

    kv_shape = lambda n, l: (n, l, N_HEADS, head_dim)
    return (yp, yp_bf, ys, ys_bf, kp.reshape(kv_shape(batch, seq)), vp.reshape(kv_shape(batch, seq)),
            ks.reshape(kv_shape(n_seq, n_new)), vs.reshape(kv_shape(n_seq, n_new)))


def _hgrn_group(x, x_bf, n_seq, s0, w_in_bf, w_o_bf, lb_logits, norm_g, layer, ln_g, ln_b, alpha):
    m, d = x.shape
    n_heads = d // HGRN_EXPAND
    seq = m // n_seq
    y = _matmul_heads(x_bf, w_in_bf, HGRN_EXPAND).reshape(4, n_heads, n_seq, seq, HGRN_EXPAND)
    if seq % HGRN_CHUNK == 0:
        chunk, lp = HGRN_CHUNK, seq
    else:
        chunk = lp = max(BF16_SUBLANES, int(2 ** math.ceil(math.log2(seq))))
        y = jnp.pad(y, ((0, 0), (0, 0), (0, 0), (0, lp - seq), (0, 0)))
    o, s = _hgrn_recurrence(y, s0, lb_logits, norm_g, layer, min(seq, chunk), chunk)
    o = o[:, :seq].reshape(m, d)
    y_out, y_out_bf = _proj_ln(o, w_o_bf, x, ln_g, ln_b, alpha)
    return y_out, y_out_bf, s


def kernel(x_prompt, x_sample, cache_k, cache_v, state_hgrn, page_table, attn_w_qkv, attn_w_o, hgrn_w_in,
           hgrn_lb_logits, hgrn_norm_g, hgrn_w_o, ffn_w_in, ffn_w_out, ln_mix_g, ln_mix_b, ln_ffn_g, ln_ffn_b):
    batch, seq, d = x_prompt.shape
    n_seq, n_new, _ = x_sample.shape
    depth = ffn_w_in.shape[0]
    alpha = (2.0 * depth) ** 0.25
    n_hgrn_heads = d // HGRN_EXPAND

    yp = x_prompt.reshape(batch * seq, d)
    ys = x_sample.reshape(n_seq * n_new, d)
    yp_bf, ys_bf = yp.astype(BF16), ys.astype(BF16)
    pk, pv, ps, sk, sv, ss = [], [], [], [], [], []
    for i in range(depth):
        if i % 2 == 0:
            a = i // 2
            yp, yp_bf, ys, ys_bf, kp, vp, kn, vn = _moba_layer(
                yp, yp_bf, ys, ys_bf, batch, n_seq, cache_k, cache_v, a, page_table,
                attn_w_qkv[a].astype(BF16), attn_w_o[a].astype(BF16), ln_mix_g[i], ln_mix_b[i], alpha)
            pk.append(kp)
            pv.append(vp)
            sk.append(kn)
            sv.append(vn)
        else:
            r = i // 2
            w_in_bf, w_o_bf = hgrn_w_in[r].astype(BF16), hgrn_w_o[r].astype(BF16)
            s0p = jnp.zeros((batch, n_hgrn_heads, HGRN_EXPAND, d // n_hgrn_heads), state_hgrn.dtype)
            yp, yp_bf, sp = _hgrn_group(yp, yp_bf, batch, s0p, w_in_bf, w_o_bf, hgrn_lb_logits, hgrn_norm_g[r],
                                        i, ln_mix_g[i], ln_mix_b[i], alpha)
            ys, ys_bf, sn = _hgrn_group(ys, ys_bf, n_seq, state_hgrn[r], w_in_bf, w_o_bf, hgrn_lb_logits,
                                        hgrn_norm_g[r], i, ln_mix_g[i], ln_mix_b[i], alpha)
            ps.append(sp)
            ss.append(sn)
        w_in_bf, w_out_bf = ffn_w_in[i].astype(BF16), ffn_w_out[i].astype(BF16)
        yp, yp_bf = _ffn_ln(yp_bf, w_in_bf, w_out_bf, yp, ln_ffn_g[i], ln_ffn_b[i], alpha)
        ys, ys_bf = _ffn_ln(ys_bf, w_in_bf, w_out_bf, ys, ln_ffn_g[i], ln_ffn_b[i], alpha)
    return (yp.reshape(batch, seq, d), ys.reshape(n_seq, n_new, d), jnp.stack(pk), jnp.stack(pv), jnp.stack(ps),
            jnp.stack(sk), jnp.stack(sv), jnp.stack(ss))
```

```python
import functools
import math

import numpy as np
import jax
import jax.numpy as jnp
from jax import lax
from jax.experimental import pallas as pl
from jax.experimental.pallas import tpu as pltpu

F32 = jnp.float32
BF16 = jnp.bfloat16
NEG_INF = float("-inf")

N_HEADS = 16
MOBA_BLOCK = 256
MOBA_TOP_K = 3
ROPE_THETA = 10000.0
HGRN_EXPAND = 128
HGRN_CHUNK = 64
LN_EPS = 1e-5
RMS_EPS = 1e-6

LANES = 128
SUBLANES = 8
BF16_SUBLANES = 16
VMEM_LIMIT_BYTES = 56 * 1024 * 1024

NT_DIMS = (((1,), (1,)), ((), ()))


def _params(*semantics):
    return pltpu.CompilerParams(dimension_semantics=semantics, vmem_limit_bytes=VMEM_LIMIT_BYTES)


def _row_tile(m, target):
    if m <= target:
        return m
    t = target
    while m % t:
        t //= 2
    return t


def _dot(a, b):
    return jnp.dot(a, b, preferred_element_type=F32)


def _dot_nt(a, b, precision=None):
    return lax.dot_general(a, b, NT_DIMS, precision=precision, preferred_element_type=F32)


def _sigmoid(x):
    return 1.0 / (1.0 + jnp.exp(-x))


def _layer_norm(z, g, b):
    mu = jnp.mean(z, axis=-1, keepdims=True)
    zc = z - mu
    var = jnp.mean(zc * zc, axis=-1, keepdims=True)
    return zc * lax.rsqrt(var + LN_EPS) * g + b


def _top_k_indices(gate, idf, n_ids, axis):
    picks = []
    for _ in range(MOBA_TOP_K):
        m = jnp.max(gate, axis=axis, keepdims=True)
        idx = jnp.min(jnp.where(gate == m, idf, float(n_ids)), axis=axis, keepdims=True)
        picks.append(jnp.where(m > NEG_INF, idx, -1.0))
        gate = jnp.where(idf == idx, NEG_INF, gate)
    return picks


def _qkv_rope_kernel(x_ref, wq_ref, wk_ref, wv_ref, cos_ref, sin_ref, q_ref, k_ref, v_ref, *, head_dim):
    x = x_ref[...]
    cos = cos_ref[...]
    sin = sin_ref[...]
    q = _dot(x, wq_ref[...])
    k = _dot(x, wk_ref[...])
    v_ref[...] = _dot(x, wv_ref[...])
    for h in range(q.shape[1] // head_dim):
        sl = slice(h * head_dim, (h + 1) * head_dim)
        qh = q[:, sl]
        kh = k[:, sl]
        q_ref[:, sl] = qh * cos + pltpu.roll(qh, head_dim // 2, axis=1) * sin
        k_ref[:, sl] = kh * cos + pltpu.roll(kh, head_dim // 2, axis=1) * sin


def _rope_tables(pos, head_dim):
    half = head_dim // 2
    inv = ROPE_THETA ** (-jnp.arange(half, dtype=F32) * (2.0 / head_dim))
    ang = pos.astype(F32)[:, None] * inv[None, :]
    cos, sin = jnp.cos(ang), jnp.sin(ang)
    return jnp.concatenate([cos, cos], -1), jnp.concatenate([-sin, sin], -1)


def _qkv_rope(x_bf, w_qkv_bf, cos, sin):
    m, d = x_bf.shape
    head_dim = d // N_HEADS
    tm = _row_tile(m, 1024)
    tn = 512
    nj = d // tn
    out = jax.ShapeDtypeStruct((m, d), F32)
    o_spec = pl.BlockSpec((tm, tn), lambda j, i: (i, j))
    return pl.pallas_call(
        functools.partial(_qkv_rope_kernel, head_dim=head_dim),
        out_shape=(out, out, out),
        grid=(nj, m // tm),
        in_specs=[
            pl.BlockSpec((tm, d), lambda j, i: (i, 0)),
            pl.BlockSpec((d, tn), lambda j, i: (0, j)),
            pl.BlockSpec((d, tn), lambda j, i: (0, nj + j)),
            pl.BlockSpec((d, tn), lambda j, i: (0, 2 * nj + j)),
            pl.BlockSpec((tm, head_dim), lambda j, i: (i, 0)),
            pl.BlockSpec((tm, head_dim), lambda j, i: (i, 0)),
        ],
        out_specs=(o_spec, o_spec, o_spec),
        compiler_params=_params("parallel", "parallel"),
        name="qkv_rope",
    )(x_bf, w_qkv_bf, w_qkv_bf, w_qkv_bf, cos, sin)


def _moba_prompt_kernel(q_ref, k_ref, v_ref, o_ref, kb_ref, vt_ref, km_ref, *, n_blocks, heads, kv_unroll, scale):
    qi = pl.program_id(2)
    blk = MOBA_BLOCK
    head_dim = q_ref.shape[1] // heads
    exp2_scale = scale * math.log2(math.e)

    @pl.when(qi == 0)
    def _():
        for h in range(heads):
            cols = slice(h * head_dim, (h + 1) * head_dim)
            for n in range(n_blocks):
                rows = slice(n * blk, (n + 1) * blk)
                kf = k_ref[rows, cols]
                km_ref[h, n:n + 1, :] = jnp.sum(kf, axis=0, keepdims=True) * (1.0 / blk)
                kb_ref[h, n] = kf.astype(BF16)
                vt_ref[h, n] = v_ref[rows, cols].T.astype(BF16)

    key_id = lax.broadcasted_iota(jnp.int32, (blk, blk), 0)
    qry_id = lax.broadcasted_iota(jnp.int32, (blk, blk), 1)
    blk_id = lax.broadcasted_iota(jnp.int32, (n_blocks, blk), 0)
    qtbs, picks, init = [], [], []
    for h in range(heads):
        qt = q_ref[:, h * head_dim:(h + 1) * head_dim].T
        gate = jnp.dot(km_ref[h], qt, precision=lax.Precision.HIGHEST, preferred_element_type=F32)
        gate = jnp.where(blk_id < qi, gate, NEG_INF)
        picks.append(_top_k_indices(gate, blk_id.astype(F32), n_blocks, axis=0))
        qtb = qt.astype(BF16)
        qtbs.append(qtb)
        st = jnp.where(key_id <= qry_id, _dot(kb_ref[h, qi], qtb), NEG_INF)
        m0 = jnp.max(st, axis=0, keepdims=True)
        p = jnp.exp2((st - m0) * exp2_scale)
        init.append((m0, jnp.sum(p, axis=0, keepdims=True), _dot(vt_ref[h, qi], p.astype(BF16))))

    def body(it, carry):
        ids = [it * kv_unroll + u for u in range(kv_unroll)]
        clamped = [jnp.minimum(n, n_blocks - 1) for n in ids]
        scores = [[_dot(kb_ref[h, nc], qtbs[h]) for nc in clamped] for h in range(heads)]
        out = []
        for h in range(heads):
            m, l, acc = carry[h]
            s1, s2, s3 = picks[h]
            blocks = []
            m_new = m
            for n, nc, sn in zip(ids, clamped, scores[h]):
                nf = n.astype(F32)
                picked = (s1 == nf) | (s2 == nf) | (s3 == nf)
                m_new = jnp.maximum(m_new, jnp.where(picked, jnp.max(sn, axis=0, keepdims=True), NEG_INF))
                blocks.append((nc, picked, sn))
            alpha = jnp.exp2((m - m_new) * exp2_scale)
            l = alpha * l
            acc = alpha * acc
            for nc, picked, sn in blocks:
                shift = jnp.where(picked, m_new, jnp.inf)
                pn = jnp.exp2((sn - shift) * exp2_scale)
                l = l + jnp.sum(pn, axis=0, keepdims=True)
                acc = acc + _dot(vt_ref[h, nc], pn.astype(BF16))
            out.append((m_new, l, acc))
        return tuple(out)

    final = lax.fori_loop(0, (qi + kv_unroll - 1) // kv_unroll, body, tuple(init))
    for h in range(heads):
        _, l, acc = final[h]
        o_ref[:, h * head_dim:(h + 1) * head_dim] = (acc / l).T.astype(o_ref.dtype)


def _moba_prompt_attention(q, k, v, batch, heads=2, kv_unroll=2):
    m, d = q.shape
    seq = m // batch
    head_dim = d // N_HEADS
    assert seq % MOBA_BLOCK == 0 and seq // MOBA_BLOCK >= MOBA_TOP_K and N_HEADS % heads == 0
    n_blocks = seq // MOBA_BLOCK
    width = heads * head_dim
    kv_spec = pl.BlockSpec((seq, width), lambda b, h, i: (b, h))
    qo_spec = pl.BlockSpec((MOBA_BLOCK, width), lambda b, h, i: (b * n_blocks + i, h))
    return pl.pallas_call(
        functools.partial(_moba_prompt_kernel, n_blocks=n_blocks, heads=heads, kv_unroll=kv_unroll,
                          scale=head_dim ** -0.5),
        out_shape=jax.ShapeDtypeStruct((m, d), BF16),
        grid=(batch, N_HEADS // heads, n_blocks),
        in_specs=[qo_spec, kv_spec, kv_spec],
        out_specs=qo_spec,
        scratch_shapes=[
            pltpu.VMEM((heads, n_blocks, MOBA_BLOCK, head_dim), BF16),
            pltpu.VMEM((heads, n_blocks, head_dim, MOBA_BLOCK), BF16),
            pltpu.VMEM((heads, n_blocks, head_dim), F32),
        ],
        compiler_params=_params("parallel", "parallel", "arbitrary"),
        name="moba_prompt_attention",
    )(q, k, v)


def _sample_stream_kernel(pt_ref, q_ref, bias_ref, *refs, blocks_per_step, pages_per_block, scale):
    del pt_ref
    n_pages = blocks_per_step * pages_per_block
    k_refs, v_refs = refs[:n_pages], refs[n_pages:2 * n_pages]
    km_ref, ml_ref, acc_ref = refs[2 * n_pages:]
    page, n_heads, head_dim = k_refs[0].shape
    exp2_scale = scale * math.log2(math.e)
    qb = q_ref[...].astype(BF16)
    bias = bias_ref[...]
    lane = lax.broadcasted_iota(jnp.int32, (qb.shape[0], head_dim), 1)
    for j in range(blocks_per_step):
        pages = range(j * pages_per_block, (j + 1) * pages_per_block)
        k_sum = None
        scores = []
        for pg in pages:
            kf = k_refs[pg][...]
            k_sum = jnp.sum(kf, axis=0) if k_sum is None else k_sum + jnp.sum(kf, axis=0)
            scores.append(_dot_nt(qb, kf.reshape(page * n_heads, head_dim).astype(BF16)) + bias)
        km_ref[j] = k_sum * (1.0 / (page * pages_per_block))
        m = functools.reduce(jnp.maximum, [jnp.max(s, axis=1, keepdims=True) for s in scores])
        l = None
        acc = None
        for pg, s in zip(pages, scores):
            p = jnp.exp2((s - m) * exp2_scale)
            pv = _dot(p.astype(BF16), v_refs[pg][...].reshape(page * n_heads, head_dim).astype(BF16))
            ps = jnp.sum(p, axis=1, keepdims=True)
            l, acc = (ps, pv) if l is None else (l + ps, acc + pv)
        acc_ref[j] = acc
        ml_ref[j] = jnp.where(lane == 0, m, jnp.where(lane == 1, l, 0.0))


def _sample_stream(q_rows, cache_k, cache_v, layer, page_table, blocks_per_step=2):
    n_seq, rows, head_dim = q_rows.shape
    page, n_heads = cache_k.shape[2], cache_k.shape[3]
    assert MOBA_BLOCK % page == 0
    ppb = MOBA_BLOCK // page
    n_blk = page_table.shape[1] // ppb
    assert n_blk % blocks_per_step == 0
    n_pages = blocks_per_step * ppb
    head_of_row = np.arange(rows)[:, None] % n_heads
    head_of_key = np.arange(page * n_heads)[None, :] % n_heads
    bias = jnp.asarray(np.where(head_of_row == head_of_key, 0.0, -np.inf), dtype=F32)

    def page_spec(j):
        return pl.BlockSpec((None, None, page, n_heads, head_dim),
                            lambda b, n, pt: (layer, pt[b, n * n_pages + j], 0, 0, 0))

    part = jax.ShapeDtypeStruct((n_seq, n_blk, rows, head_dim), F32)
    part_spec = pl.BlockSpec((None, blocks_per_step, rows, head_dim), lambda b, n, pt: (b, n, 0, 0))
    page_specs = [page_spec(j) for j in range(n_pages)]
    return pl.pallas_call(
        functools.partial(_sample_stream_kernel, blocks_per_step=blocks_per_step, pages_per_block=ppb,
                          scale=head_dim ** -0.5),
        out_shape=(jax.ShapeDtypeStruct((n_seq, n_blk, n_heads, head_dim), F32), part, part),
        grid_spec=pltpu.PrefetchScalarGridSpec(
            num_scalar_prefetch=1,
            grid=(n_seq, n_blk // blocks_per_step),
            in_specs=[
                pl.BlockSpec((None, rows, head_dim), lambda b, n, pt: (b, 0, 0)),
                pl.BlockSpec(bias.shape, lambda b, n, pt: (0, 0)),
            ] + page_specs + page_specs,
            out_specs=(
                pl.BlockSpec((None, blocks_per_step, n_heads, head_dim), lambda b, n, pt: (b, n, 0, 0)),
                part_spec, part_spec,
            ),
        ),
        compiler_params=_params("parallel", "arbitrary"),
        name="sample_cache_stream",
    )(page_table, q_rows, bias, *([cache_k] * n_pages), *([cache_v] * n_pages))


def _sample_combine_kernel(q_ref, km_ref, kn_ref, vn_ref, ml_ref, acc_ref, o_ref, kmx_ref,
                           *, n_heads, q_blk, scale):
    n_blk = ml_ref.shape[0]
    n_past = n_blk * n_heads
    qf = q_ref[...]
    kn = kn_ref[...]
    rows, head_dim = qf.shape

    kmx_ref[0:n_past, :] = km_ref[...]
    own_sum = jnp.sum(kn.reshape(rows // n_heads, n_heads, head_dim), axis=0)
    kmx_ref[n_past:n_past + n_heads, :] = own_sum * (1.0 / MOBA_BLOCK)
    gate = _dot_nt(qf, kmx_ref[...], precision=lax.Precision.HIGHEST)
    col = lax.broadcasted_iota(jnp.int32, gate.shape, 1)
    row = lax.broadcasted_iota(jnp.int32, gate.shape, 0)
    valid = ((col % n_heads) == (row % n_heads)) & ((col // n_heads) < q_blk)
    gate = jnp.where(valid, gate, NEG_INF)
    s1, s2, s3 = _top_k_indices(gate, col.astype(F32), gate.shape[1], axis=1)
    head_of_row = (lax.broadcasted_iota(jnp.int32, (rows, 1), 0) % n_heads).astype(F32)

    def picked(n):
        tgt = n.astype(F32) * float(n_heads) + head_of_row
        return (s1 == tgt) | (s2 == tgt) | (s3 == tgt)

    so = _dot_nt(qf.astype(BF16), kn.astype(BF16)) * scale
    r2 = lax.broadcasted_iota(jnp.int32, so.shape, 0)
    c2 = lax.broadcasted_iota(jnp.int32, so.shape, 1)
    ok = ((c2 % n_heads) == (r2 % n_heads)) & ((c2 // n_heads) <= (r2 // n_heads))
    so = jnp.where(ok, so, NEG_INF)
    m_own = jnp.max(so, axis=1, keepdims=True)

    def max_body(n, m):
        return jnp.maximum(m, jnp.where(picked(n), ml_ref[n][:, 0:1] * scale, NEG_INF))

    m_all = lax.fori_loop(0, n_blk, max_body, m_own)
    p_own = jnp.exp(so - m_all)
    l_own = jnp.sum(p_own, axis=1, keepdims=True)
    acc_own = _dot(p_own.astype(BF16), vn_ref[...].astype(BF16))

    def sum_body(n, carry):
        l, acc = carry
        ml = ml_ref[n]
        w = jnp.where(picked(n), jnp.exp(ml[:, 0:1] * scale - m_all), 0.0)
        return l + w * ml[:, 1:2], acc + w * acc_ref[n]

    l, acc = lax.fori_loop(0, n_blk, sum_body, (l_own, acc_own))
    o_ref[...] = (acc / l).astype(o_ref.dtype)


def _sample_combine(q_rows, km, k_rows, v_rows, ml, acc, q_blk):
    n_seq, rows, head_dim = q_rows.shape
    n_blk, n_heads = km.shape[1], km.shape[2]
    seq_spec = pl.BlockSpec((None, rows, head_dim), lambda b: (b, 0, 0))
    part_spec = pl.BlockSpec((None, n_blk, rows, head_dim), lambda b: (b, 0, 0, 0))
    return pl.pallas_call(
        functools.partial(_sample_combine_kernel, n_heads=n_heads, q_blk=q_blk, scale=head_dim ** -0.5),
        out_shape=jax.ShapeDtypeStruct((n_seq, rows, head_dim), BF16),
        grid=(n_seq,),
        in_specs=[
            seq_spec,
            pl.BlockSpec((None, n_blk * n_heads, head_dim), lambda b: (b, 0, 0)),
            seq_spec, seq_spec, part_spec, part_spec,
        ],
        out_specs=seq_spec,
        scratch_shapes=[pltpu.VMEM(((n_blk + 1) * n_heads, head_dim), F32)],
        compiler_params=_params("parallel"),
        name="sample_select_combine",
    )(q_rows, km.reshape(n_seq, n_blk * n_heads, head_dim), k_rows, v_rows, ml, acc)


def _proj_ln_kernel(x_ref, w_ref, r_ref, g_ref, b_ref, y_ref, yb_ref, *, alpha):
    z = alpha * r_ref[...] + _dot(x_ref[...], w_ref[...])
    y = _layer_norm(z, g_ref[...], b_ref[...])
    y_ref[...] = y
    yb_ref[...] = y.astype(BF16)


def _proj_ln(x_bf, w_bf, resid, g, b, alpha):
    m, d_in = x_bf.shape
    d = w_bf.shape[1]
    tm = _row_tile(m, 256)
    row = lambda width: pl.BlockSpec((tm, width), lambda i: (i, 0))
    vec = pl.BlockSpec((1, d), lambda i: (0, 0))
    return pl.pallas_call(
        functools.partial(_proj_ln_kernel, alpha=alpha),
        out_shape=(jax.ShapeDtypeStruct((m, d), F32), jax.ShapeDtypeStruct((m, d), BF16)),
        grid=(m // tm,),
        in_specs=[row(d_in), pl.BlockSpec((d_in, d), lambda i: (0, 0)), row(d), vec, vec],
        out_specs=(row(d), row(d)),
        compiler_params=_params("parallel"),
        name="proj_residual_ln",
    )(x_bf, w_bf, resid, g.reshape(1, d), b.reshape(1, d))


def _ffn_ln_kernel(x_ref, wg_ref, wu_ref, wo_ref, r_ref, g_ref, b_ref, y_ref, yb_ref, acc_ref, *, alpha):
    f = pl.program_id(1)
    x = x_ref[...]
    gate = _dot(x, wg_ref[...])
    up = _dot(x, wu_ref[...])
    hidden = (gate * _sigmoid(gate) * up).astype(BF16)
    part = _dot(hidden, wo_ref[...])

    @pl.when(f == 0)
    def _():
        acc_ref[...] = part

    @pl.when(f > 0)
    def _():
        acc_ref[...] += part

    @pl.when(f == pl.num_programs(1) - 1)
    def _():
        y = _layer_norm(alpha * r_ref[...] + acc_ref[...], g_ref[...], b_ref[...])
        y_ref[...] = y
        yb_ref[...] = y.astype(BF16)


def _ffn_ln(x_bf, w_in_bf, w_out_bf, resid, g, b, alpha):
    m, d = x_bf.shape
    d_ff = w_out_bf.shape[0]
    tm = _row_tile(m, 512)
    tf = 512
    assert d_ff % tf == 0
    nf = d_ff // tf
    row = lambda: pl.BlockSpec((tm, d), lambda i, f: (i, 0))
    vec = pl.BlockSpec((1, d), lambda i, f: (0, 0))
    return pl.pallas_call(
        functools.partial(_ffn_ln_kernel, alpha=alpha),
        out_shape=(jax.ShapeDtypeStruct((m, d), F32), jax.ShapeDtypeStruct((m, d), BF16)),
        grid=(m // tm, nf),
        in_specs=[
            row(),
            pl.BlockSpec((d, tf), lambda i, f: (0, f)),
            pl.BlockSpec((d, tf), lambda i, f: (0, nf + f)),
            pl.BlockSpec((tf, d), lambda i, f: (f, 0)),
            row(), vec, vec,
        ],
        out_specs=(row(), row()),
        scratch_shapes=[pltpu.VMEM((tm, d), F32)],
        compiler_params=_params("parallel", "arbitrary"),
        name="swiglu_residual_ln",
    )(x_bf, w_in_bf, w_in_bf, w_out_bf, resid, g.reshape(1, d), b.reshape(1, d))


def _matmul_kernel(x_ref, w_ref, o_ref):
    o_ref[...] = _dot(x_ref[...], w_ref[...])


def _matmul(x_bf, w_bf):
    m, d = x_bf.shape
    n_out = w_bf.shape[1]
    tm = _row_tile(m, 1024)
    tn = 512
    return pl.pallas_call(
        _matmul_kernel,
        out_shape=jax.ShapeDtypeStruct((m, n_out), F32),
        grid=(n_out // tn, m // tm),
        in_specs=[pl.BlockSpec((tm, d), lambda j, i: (i, 0)), pl.BlockSpec((d, tn), lambda j, i: (0, j))],
        out_specs=pl.BlockSpec((tm, tn), lambda j, i: (i, j)),
        compiler_params=_params("parallel", "parallel"),
        name="hgrn_in_proj",
    )(x_bf, w_bf)


def _hgrn_level_matrix(chunk):
    n_lev = int(math.log2(chunk))
    assert 2 ** n_lev == chunk
    t = np.arange(chunk)[:, None]
    s = np.arange(chunk)[None, :]
    mats = [s <= t]
    for lev in range(1, n_lev + 1):
        size = 2 ** lev
        mats.append(s < (t // size) * size + size // 2)
    return np.concatenate(mats, 0).astype(np.float32), n_lev


def _hgrn_kernel(q_ref, f_ref, i_ref, g_ref, lbl_ref, ng_ref, cm_ref, s0_ref, o_ref, s_ref, st_ref,
                 *, layer, n_valid, n_lev, n_heads):
    c = pl.program_id(1)
    chunk, d = q_ref.shape
    dk = d // n_heads

    @pl.when(c == 0)
    def _():
        for h in range(n_heads):
            st_ref[h] = s0_ref[h].T

    logits = lbl_ref[...]
    lrow = lax.broadcasted_iota(jnp.int32, logits.shape, 0)
    e = jnp.exp(logits - jnp.max(logits, axis=0, keepdims=True))
    sm = e / jnp.sum(e, axis=0, keepdims=True)
    lb = jnp.sum(jnp.where((lrow >= 1) & (lrow <= layer), sm, 0.0), axis=0, keepdims=True)

    row = lax.broadcasted_iota(jnp.int32, (chunk, d), 0)
    qraw = q_ref[...]
    q = qraw * _sigmoid(qraw)
    f = lb + (1.0 - lb) * _sigmoid(f_ref[...])
    k = 1.0 - f
    g = jnp.log(f)
    if n_valid < chunk:
        live = row < n_valid
        k = jnp.where(live, k, 0.0)
        g = jnp.where(live, g, 0.0)
    v = i_ref[...]
    vb = v.astype(BF16)
    graw = g_ref[...]
    out_gate = graw * _sigmoid(graw)

    cm = cm_ref[...]
    g1 = g.astype(BF16)
    r1 = g - g1.astype(F32)
    g2 = r1.astype(BF16)
    g3 = (r1 - g2.astype(F32)).astype(BF16)
    b_all = _dot(cm, g1) + _dot(cm, g2) + _dot(cm, g3)
    b = b_all[0:chunk]

    t_id = lax.broadcasted_iota(jnp.int32, (chunk, chunk), 0)
    s_id = lax.broadcasted_iota(jnp.int32, (chunk, chunk), 1)
    qb = q.astype(BF16)
    kb = k.astype(BF16)
    factors, pairs = [], []
    for lev in range(1, n_lev + 1):
        size = 2 ** lev
        half = size // 2
        decay = jnp.exp(-jnp.abs(b - b_all[lev * chunk:(lev + 1) * chunk]))
        factors.append((jnp.where((row % size) >= half, q, k) * decay).astype(BF16))
        pairs.append(((t_id // size) == (s_id // size)) & ((t_id % size) >= half) & ((s_id % size) < half))

    q_decayed = (q * jnp.exp(b)).astype(BF16)
    b_last = b[chunk - 1:chunk, :]
    k_decayed = (k * jnp.exp(b_last - b)).astype(BF16)
    state_decay = jnp.exp(b_last)
    ng = ng_ref[...]

    for h in range(n_heads):
        hs = slice(h * dk, (h + 1) * dk)
        a = jnp.where(t_id == s_id, _dot_nt(qb[:, hs], kb[:, hs]), 0.0)
        for x, pair in zip(factors, pairs):
            a = a + jnp.where(pair, _dot_nt(x[:, hs], x[:, hs]), 0.0)
        st = st_ref[h]
        o = _dot_nt(q_decayed[:, hs], st.astype(BF16)) + _dot(a.astype(BF16), vb[:, hs])
        st_ref[h] = st * state_decay[:, hs] + _dot(v[:, hs].T.astype(BF16), k_decayed[:, hs])
        o = o * lax.rsqrt(jnp.mean(o * o, axis=-1, keepdims=True) + RMS_EPS) * ng
        o_ref[:, hs] = (o * out_gate[:, hs]).astype(o_ref.dtype)

    @pl.when(c == pl.num_programs(1) - 1)
    def _():
        for h in range(n_heads):
            s_ref[h] = st_ref[h].T


def _hgrn_recurrence(y, s0, lb_logits, norm_g, layer, n_valid, chunk):
    batch, lp, d4 = y.shape
    d = d4 // 4
    _, n_heads, dk, dv = s0.shape
    assert n_heads * dk == d and dk == dv
    depth = lb_logits.shape[0]
    cm_np, n_lev = _hgrn_level_matrix(chunk)
    cm = jnp.asarray(cm_np, dtype=BF16)

    def quarter(qt):
        return pl.BlockSpec((None, chunk, d), lambda b, c: (b, c, qt))

    state_spec = pl.BlockSpec((None, n_heads, dk, dv), lambda b, c: (b, 0, 0, 0))
    return pl.pallas_call(
        functools.partial(_hgrn_kernel, layer=layer, n_valid=n_valid, n_lev=n_lev, n_heads=n_heads),
        out_shape=(jax.ShapeDtypeStruct((batch, lp, d), BF16),
                   jax.ShapeDtypeStruct((batch, n_heads, dk, dv), F32)),
        grid=(batch, lp // chunk),
        in_specs=[
            quarter(0), quarter(1), quarter(2), quarter(3),
            pl.BlockSpec((depth, d), lambda b, c: (0, 0)),
            pl.BlockSpec((1, dv), lambda b, c: (0, 0)),
            pl.BlockSpec(cm.shape, lambda b, c: (0, 0)),
            state_spec,
        ],
        out_specs=(pl.BlockSpec((None, chunk, d), lambda b, c: (b, c, 0)), state_spec),
        scratch_shapes=[pltpu.VMEM((n_heads, dv, dk), F32)],
        compiler_params=_params("parallel", "arbitrary"),
        name="hgrn_recurrence",
    )(y, y, y, y, lb_logits, norm_g.reshape(1, dv), cm, s0)


def _moba_layer(xp, xp_bf, xs, xs_bf, batch, n_seq, cache_k, cache_v, layer, page_table, w_qkv_bf, w_o_bf,
                ln_g, ln_b, alpha):
    d = xp.shape[-1]
    head_dim = d // N_HEADS
    seq = xp.shape[0] // batch
    n_new = xs.shape[0] // n_seq
    past = page_table.shape[1] * cache_k.shape[2]
    assert past % MOBA_BLOCK == 0 and n_new <= MOBA_BLOCK and cache_k.shape[3] == N_HEADS

    cos_p, sin_p = _rope_tables(jnp.arange(seq, dtype=jnp.int32), head_dim)
    qp, kp, vp = _qkv_rope(xp_bf, w_qkv_bf, jnp.tile(cos_p, (batch, 1)), jnp.tile(sin_p, (batch, 1)))
    op = _moba_prompt_attention(qp, kp, vp, batch)
    yp, yp_bf = _proj_ln(op, w_o_bf, xp, ln_g, ln_b, alpha)

    cos_s, sin_s = _rope_tables(past + jnp.arange(n_new, dtype=jnp.int32), head_dim)
    qs, ks, vs = _qkv_rope(xs_bf, w_qkv_bf, jnp.tile(cos_s, (n_seq, 1)), jnp.tile(sin_s, (n_seq, 1)))
    rows = lambda a: a.reshape(n_seq, n_new * N_HEADS, head_dim)
    km, ml, acc = _sample_stream(rows(qs), cache_k, cache_v, layer, page_table)
    os_ = _sample_combine(rows(qs), km, rows(ks), rows(vs), ml, acc, past // MOBA_BLOCK)
    ys, ys_bf = _proj_ln(os_.reshape(n_seq * n_new, d), w_o_bf, xs, ln_g, ln_b, alpha)

    kv_shape = lambda n, l: (n, l, N_HEADS, head_dim)
    return (yp, yp_bf, ys, ys_bf, kp.reshape(kv_shape(batch, seq)), vp.reshape(kv_shape(batch, seq)),
            ks.reshape(kv_shape(n_seq, n_new)), vs.reshape(kv_shape(n_seq, n_new)))


def _hgrn_group(x, x_bf, n_seq, s0, w_in_bf, w_o_bf, lb_logits, norm_g, layer, ln_g, ln_b, alpha):
    m, d = x.shape
    n_heads = d // HGRN_EXPAND
    seq = m // n_seq
    y = _matmul(x_bf, w_in_bf).reshape(n_seq, seq, 4 * d)
    if seq % HGRN_CHUNK == 0:
        chunk, lp = HGRN_CHUNK, seq
    else:
        chunk = lp = max(BF16_SUBLANES, int(2 ** math.ceil(math.log2(seq))))
        y = jnp.pad(y, ((0, 0), (0, lp - seq), (0, 0)))
    o, s = _hgrn_recurrence(y, s0, lb_logits, norm_g, layer, min(seq, chunk), chunk)
    o = o[:, :seq].reshape(m, d)
    y_out, y_out_bf = _proj_ln(o, w_o_bf, x, ln_g, ln_b, alpha)
    return y_out, y_out_bf, s


def kernel(x_prompt, x_sample, cache_k, cache_v, state_hgrn, page_table, attn_w_qkv, attn_w_o, hgrn_w_in,
           hgrn_lb_logits, hgrn_norm_g, hgrn_w_o, ffn_w_in, ffn_w_out, ln_mix_g, ln_mix_b, ln_ffn_g, ln_ffn_b):
    batch, seq, d = x_prompt.shape
    n_seq, n_new, _ = x_sample.shape
    depth = ffn_w_in.shape[0]
    alpha = (2.0 * depth) ** 0.25
    n_hgrn_heads = d // HGRN_EXPAND

    yp = x_prompt.reshape(batch * seq, d)
    ys = x_sample.reshape(n_seq * n_new, d)
    yp_bf, ys_bf = yp.astype(BF16), ys.astype(BF16)
    pk, pv, ps, sk, sv, ss = [], [], [], [], [], []
    for i in range(depth):
        if i % 2 == 0:
            a = i // 2
            yp, yp_bf, ys, ys_bf, kp, vp, kn, vn = _moba_layer(
                yp, yp_bf, ys, ys_bf, batch, n_seq, cache_k, cache_v, a, page_table,
                attn_w_qkv[a].astype(BF16), attn_w_o[a].astype(BF16), ln_mix_g[i], ln_mix_b[i], alpha)
            pk.append(kp)
            pv.append(vp)
            sk.append(kn)
            sv.append(vn)
        else:
            r = i // 2
            w_in_bf, w_o_bf = hgrn_w_in[r].astype(BF16), hgrn_w_o[r].astype(BF16)
            s0p = jnp.zeros((batch, n_hgrn_heads, HGRN_EXPAND, d // n_hgrn_heads), state_hgrn.dtype)
            yp, yp_bf, sp = _hgrn_group(yp, yp_bf, batch, s0p, w_in_bf, w_o_bf, hgrn_lb_logits, hgrn_norm_g[r],
                                        i, ln_mix_g[i], ln_mix_b[i], alpha)
            ys, ys_bf, sn = _hgrn_group(ys, ys_bf, n_seq, state_hgrn[r], w_in_bf, w_o_bf, hgrn_lb_logits,
                                        hgrn_norm_g[r], i, ln_mix_g[i], ln_mix_b[i], alpha)
            ps.append(sp)
            ss.append(sn)
        w_in_bf, w_out_bf = ffn_w_in[i].astype(BF16), ffn_w_out[i].astype(BF16)
        yp, yp_bf = _ffn_ln(yp_bf, w_in_bf, w_out_bf, yp, ln_ffn_g[i], ln_ffn_b[i], alpha)
        ys, ys_bf = _ffn_ln(ys_bf, w_in_bf, w_out_bf, ys, ln_ffn_g[i], ln_ffn_b[i], alpha)
    return (yp.reshape(batch, seq, d), ys.reshape(n_seq, n_new, d), jnp.stack(pk), jnp.stack(pv), jnp.stack(ps),
            jnp.stack(sk), jnp.stack(sv), jnp.stack(ss))
```

```python
import functools
import math

import numpy as np
import jax
import jax.numpy as jnp
from jax import lax
from jax.experimental import pallas as pl
from jax.experimental.pallas import tpu as pltpu

F32 = jnp.float32
BF16 = jnp.bfloat16
NEG_INF = float("-inf")

N_HEADS = 16
MOBA_BLOCK = 256
MOBA_TOP_K = 3
ROPE_THETA = 10000.0
HGRN_EXPAND = 128
HGRN_CHUNK = 128
LN_EPS = 1e-5
RMS_EPS = 1e-6

LANES = 128
SUBLANES = 8
BF16_SUBLANES = 16
VMEM_LIMIT_BYTES = 56 * 1024 * 1024

NT_DIMS = (((1,), (1,)), ((), ()))


def _params(*semantics):
    return pltpu.CompilerParams(dimension_semantics=semantics, vmem_limit_bytes=VMEM_LIMIT_BYTES)


def _row_tile(m, target):
    if m <= target:
        return m
    t = target
    while m % t:
        t //= 2
    return t


def _dot(a, b):
    return jnp.dot(a, b, preferred_element_type=F32)


def _dot_nt(a, b, precision=None):
    return lax.dot_general(a, b, NT_DIMS, precision=precision, preferred_element_type=F32)


def _sigmoid(x):
    return 1.0 / (1.0 + jnp.exp(-x))


def _layer_norm(z, g, b):
    mu = jnp.mean(z, axis=-1, keepdims=True)
    zc = z - mu
    var = jnp.mean(zc * zc, axis=-1, keepdims=True)
    return zc * lax.rsqrt(var + LN_EPS) * g + b


def _top_k_indices(gate, idf, n_ids, axis):
    picks = []
    for _ in range(MOBA_TOP_K):
        m = jnp.max(gate, axis=axis, keepdims=True)
        idx = jnp.min(jnp.where(gate == m, idf, float(n_ids)), axis=axis, keepdims=True)
        picks.append(jnp.where(m > NEG_INF, idx, -1.0))
        gate = jnp.where(idf == idx, NEG_INF, gate)
    return picks


def _qkv_rope_kernel(x_ref, wq_ref, wk_ref, wv_ref, cos_ref, sin_ref, q_ref, k_ref, v_ref, *, head_dim):
    x = x_ref[...]
    cos = cos_ref[...]
    sin = sin_ref[...]
    q = _dot(x, wq_ref[...])
    k = _dot(x, wk_ref[...])
    v_ref[...] = _dot(x, wv_ref[...])
    for h in range(q.shape[1] // head_dim):
        sl = slice(h * head_dim, (h + 1) * head_dim)
        qh = q[:, sl]
        kh = k[:, sl]
        q_ref[:, sl] = qh * cos + pltpu.roll(qh, head_dim // 2, axis=1) * sin
        k_ref[:, sl] = kh * cos + pltpu.roll(kh, head_dim // 2, axis=1) * sin


def _rope_tables(pos, head_dim):
    half = head_dim // 2
    inv = ROPE_THETA ** (-jnp.arange(half, dtype=F32) * (2.0 / head_dim))
    ang = pos.astype(F32)[:, None] * inv[None, :]
    cos, sin = jnp.cos(ang), jnp.sin(ang)
    return jnp.concatenate([cos, cos], -1), jnp.concatenate([-sin, sin], -1)


def _qkv_rope(x_bf, w_qkv_bf, cos, sin):
    m, d = x_bf.shape
    head_dim = d // N_HEADS
    tm = _row_tile(m, 1024)
    tn = 512
    nj = d // tn
    out = jax.ShapeDtypeStruct((m, d), F32)
    o_spec = pl.BlockSpec((tm, tn), lambda j, i: (i, j))
    return pl.pallas_call(
        functools.partial(_qkv_rope_kernel, head_dim=head_dim),
        out_shape=(out, out, out),
        grid=(nj, m // tm),
        in_specs=[
            pl.BlockSpec((tm, d), lambda j, i: (i, 0)),
            pl.BlockSpec((d, tn), lambda j, i: (0, j)),
            pl.BlockSpec((d, tn), lambda j, i: (0, nj + j)),
            pl.BlockSpec((d, tn), lambda j, i: (0, 2 * nj + j)),
            pl.BlockSpec((tm, head_dim), lambda j, i: (i, 0)),
            pl.BlockSpec((tm, head_dim), lambda j, i: (i, 0)),
        ],
        out_specs=(o_spec, o_spec, o_spec),
        compiler_params=_params("parallel", "parallel"),
        name="qkv_rope",
    )(x_bf, w_qkv_bf, w_qkv_bf, w_qkv_bf, cos, sin)


def _moba_prompt_kernel(q_ref, k_ref, v_ref, o_ref, kb_ref, vt_ref, km_ref, p_ref,
                        *, n_blocks, heads, kv_unroll, scale):
    qi = pl.program_id(2)
    blk = MOBA_BLOCK
    head_dim = q_ref.shape[1] // heads
    exp2_scale = scale * math.log2(math.e)

    @pl.when(qi == 0)
    def _():
        for h in range(heads):
            cols = slice(h * head_dim, (h + 1) * head_dim)
            for n in range(n_blocks):
                rows = slice(n * blk, (n + 1) * blk)
                kf = k_ref[rows, cols]
                km_ref[h, n:n + 1, :] = jnp.sum(kf, axis=0, keepdims=True) * (1.0 / blk)
                kb_ref[h, n] = kf.astype(BF16)
                vt_ref[h, n] = v_ref[rows, cols].T.astype(BF16)

    key_id = lax.broadcasted_iota(jnp.int32, (blk, blk), 0)
    qry_id = lax.broadcasted_iota(jnp.int32, (blk, blk), 1)
    blk_id = lax.broadcasted_iota(jnp.int32, (n_blocks, blk), 0)
    qtbs, picks, init = [], [], []
    for h in range(heads):
        qt = q_ref[:, h * head_dim:(h + 1) * head_dim].T
        gate = jnp.dot(km_ref[h], qt, precision=lax.Precision.HIGHEST, preferred_element_type=F32)
        gate = jnp.where(blk_id < qi, gate, NEG_INF)
        picks.append(_top_k_indices(gate, blk_id.astype(F32), n_blocks, axis=0))
        qtb = qt.astype(BF16)
        qtbs.append(qtb)
        st = jnp.where(key_id <= qry_id, _dot(kb_ref[h, qi], qtb), NEG_INF)
        m0 = jnp.max(st, axis=0, keepdims=True)
        p = jnp.exp2((st - m0) * exp2_scale)
        p_ref[h, 0] = p.astype(BF16)
        for u in range(1, kv_unroll):
            p_ref[h, u] = jnp.zeros((blk, blk), BF16)
        init.append((m0, jnp.sum(p, axis=0, keepdims=True), jnp.zeros((head_dim, blk), F32)))

    def pending_values(h, pending):
        return functools.reduce(lambda a, b: a + b, [_dot(vt_ref[h, nc], p_ref[h, u]) for u, nc in enumerate(pending)])

    def body(it, carry):
        state, pending = carry
        ids = [it * kv_unroll + u for u in range(kv_unroll)]
        clamped = [jnp.minimum(n, n_blocks - 1) for n in ids]
        scores = [[_dot(kb_ref[h, nc], qtbs[h]) for nc in clamped] for h in range(heads)]
        flushed = [pending_values(h, pending) for h in range(heads)]
        out = []
        for h in range(heads):
            m, l, acc = state[h]
            s1, s2, s3 = picks[h]
            m_new = m
            blocks = []
            for n, sn in zip(ids, scores[h]):
                nf = jnp.asarray(n, F32)
                picked = (s1 == nf) | (s2 == nf) | (s3 == nf)
                m_new = jnp.maximum(m_new, jnp.where(picked, jnp.max(sn, axis=0, keepdims=True), NEG_INF))
                blocks.append((picked, sn))
            alpha = jnp.exp2((m - m_new) * exp2_scale)
            l = alpha * l
            for u, (picked, sn) in enumerate(blocks):
                shift = jnp.where(picked, m_new, jnp.inf)
                pn = jnp.exp2((sn - shift) * exp2_scale)
                l = l + jnp.sum(pn, axis=0, keepdims=True)
                p_ref[h, u] = pn.astype(BF16)
            out.append((m_new, l, alpha * (acc + flushed[h])))
        return tuple(out), tuple(clamped)

    own = tuple(qi for _ in range(kv_unroll))
    final, pending = lax.fori_loop(0, (qi + kv_unroll - 1) // kv_unroll, body, (tuple(init), own))
    for h in range(heads):
        _, l, acc = final[h]
        o = (acc + pending_values(h, pending)) / l
        o_ref[:, h * head_dim:(h + 1) * head_dim] = o.T.astype(o_ref.dtype)


def _moba_prompt_attention(q, k, v, batch, heads=2, kv_unroll=2):
    m, d = q.shape
    seq = m // batch
    head_dim = d // N_HEADS
    assert seq % MOBA_BLOCK == 0 and seq // MOBA_BLOCK >= MOBA_TOP_K and N_HEADS % heads == 0
    n_blocks = seq // MOBA_BLOCK
    width = heads * head_dim
    kv_spec = pl.BlockSpec((seq, width), lambda b, h, i: (b, h))
    qo_spec = pl.BlockSpec((MOBA_BLOCK, width), lambda b, h, i: (b * n_blocks + i, h))
    return pl.pallas_call(
        functools.partial(_moba_prompt_kernel, n_blocks=n_blocks, heads=heads, kv_unroll=kv_unroll,
                          scale=head_dim ** -0.5),
        out_shape=jax.ShapeDtypeStruct((m, d), BF16),
        grid=(batch, N_HEADS // heads, n_blocks),
        in_specs=[qo_spec, kv_spec, kv_spec],
        out_specs=qo_spec,
        scratch_shapes=[
            pltpu.VMEM((heads, n_blocks, MOBA_BLOCK, head_dim), BF16),
            pltpu.VMEM((heads, n_blocks, head_dim, MOBA_BLOCK), BF16),
            pltpu.VMEM((heads, n_blocks, head_dim), F32),
            pltpu.VMEM((heads, kv_unroll, MOBA_BLOCK, MOBA_BLOCK), BF16),
        ],
        compiler_params=_params("parallel", "parallel", "arbitrary"),
        name="moba_prompt_attention",
    )(q, k, v)


def _sample_stream_kernel(pt_ref, q_ref, bias_ref, *refs, blocks_per_step, pages_per_block, scale):
    del pt_ref
    n_pages = blocks_per_step * pages_per_block
    k_refs, v_refs = refs[:n_pages], refs[n_pages:2 * n_pages]
    km_ref, ml_ref, acc_ref = refs[2 * n_pages:]
    page, n_heads, head_dim = k_refs[0].shape
    exp2_scale = scale * math.log2(math.e)
    qb = q_ref[...].astype(BF16)
    bias = bias_ref[...]
    lane = lax.broadcasted_iota(jnp.int32, (qb.shape[0], head_dim), 1)
    for j in range(blocks_per_step):
        pages = range(j * pages_per_block, (j + 1) * pages_per_block)
        k_sum = None
        scores = []
        for pg in pages:
            kf = k_refs[pg][...]
            k_sum = jnp.sum(kf, axis=0) if k_sum is None else k_sum + jnp.sum(kf, axis=0)
            scores.append(_dot_nt(qb, kf.reshape(page * n_heads, head_dim).astype(BF16)) + bias)
        km_ref[j] = k_sum * (1.0 / (page * pages_per_block))
        m = functools.reduce(jnp.maximum, [jnp.max(s, axis=1, keepdims=True) for s in scores])
        l = None
        acc = None
        for pg, s in zip(pages, scores):
            p = jnp.exp2((s - m) * exp2_scale)
            pv = _dot(p.astype(BF16), v_refs[pg][...].reshape(page * n_heads, head_dim).astype(BF16))
            ps = jnp.sum(p, axis=1, keepdims=True)
            l, acc = (ps, pv) if l is None else (l + ps, acc + pv)
        acc_ref[j] = acc
        ml_ref[j] = jnp.where(lane == 0, m, jnp.where(lane == 1, l, 0.0))


def _sample_stream(q_rows, cache_k, cache_v, layer, page_table, blocks_per_step=4):
    n_seq, rows, head_dim = q_rows.shape
    page, n_heads = cache_k.shape[2], cache_k.shape[3]
    assert MOBA_BLOCK % page == 0
    ppb = MOBA_BLOCK // page
    n_blk = page_table.shape[1] // ppb
    assert n_blk % blocks_per_step == 0
    n_pages = blocks_per_step * ppb
    head_of_row = np.arange(rows)[:, None] % n_heads
    head_of_key = np.arange(page * n_heads)[None, :] % n_heads
    bias = jnp.asarray(np.where(head_of_row == head_of_key, 0.0, -np.inf), dtype=F32)

    def page_spec(j):
        return pl.BlockSpec((None, None, page, n_heads, head_dim),
                            lambda b, n, pt: (layer, pt[b, n * n_pages + j], 0, 0, 0))

    part = jax.ShapeDtypeStruct((n_seq, n_blk, rows, head_dim), F32)
    part_spec = pl.BlockSpec((None, blocks_per_step, rows, head_dim), lambda b, n, pt: (b, n, 0, 0))
    page_specs = [page_spec(j) for j in range(n_pages)]
    return pl.pallas_call(
        functools.partial(_sample_stream_kernel, blocks_per_step=blocks_per_step, pages_per_block=ppb,
                          scale=head_dim ** -0.5),
        out_shape=(jax.ShapeDtypeStruct((n_seq, n_blk, n_heads, head_dim), F32), part, part),
        grid_spec=pltpu.PrefetchScalarGridSpec(
            num_scalar_prefetch=1,
            grid=(n_seq, n_blk // blocks_per_step),
            in_specs=[
                pl.BlockSpec((None, rows, head_dim), lambda b, n, pt: (b, 0, 0)),
                pl.BlockSpec(bias.shape, lambda b, n, pt: (0, 0)),
            ] + page_specs + page_specs,
            out_specs=(
                pl.BlockSpec((None, blocks_per_step, n_heads, head_dim), lambda b, n, pt: (b, n, 0, 0)),
                part_spec, part_spec,
            ),
        ),
        compiler_params=_params("parallel", "arbitrary"),
        name="sample_cache_stream",
    )(page_table, q_rows, bias, *([cache_k] * n_pages), *([cache_v] * n_pages))


def _sample_combine_kernel(q_ref, km_ref, kn_ref, vn_ref, ml_ref, acc_ref, o_ref, kmx_ref,
                           *, n_heads, q_blk, scale):
    n_blk = ml_ref.shape[0]
    n_past = n_blk * n_heads
    qf = q_ref[...]
    kn = kn_ref[...]
    rows, head_dim = qf.shape

    kmx_ref[0:n_past, :] = km_ref[...]
    own_sum = jnp.sum(kn.reshape(rows // n_heads, n_heads, head_dim), axis=0)
    kmx_ref[n_past:n_past + n_heads, :] = own_sum * (1.0 / MOBA_BLOCK)
    gate = _dot_nt(qf, kmx_ref[...], precision=lax.Precision.HIGHEST)
    col = lax.broadcasted_iota(jnp.int32, gate.shape, 1)
    row = lax.broadcasted_iota(jnp.int32, gate.shape, 0)
    valid = ((col % n_heads) == (row % n_heads)) & ((col // n_heads) < q_blk)
    gate = jnp.where(valid, gate, NEG_INF)
    s1, s2, s3 = _top_k_indices(gate, col.astype(F32), gate.shape[1], axis=1)
    head_of_row = (lax.broadcasted_iota(jnp.int32, (rows, 1), 0) % n_heads).astype(F32)

    def picked(n):
        tgt = jnp.asarray(n, F32) * float(n_heads) + head_of_row
        return (s1 == tgt) | (s2 == tgt) | (s3 == tgt)

    so = _dot_nt(qf.astype(BF16), kn.astype(BF16)) * scale
    r2 = lax.broadcasted_iota(jnp.int32, so.shape, 0)
    c2 = lax.broadcasted_iota(jnp.int32, so.shape, 1)
    ok = ((c2 % n_heads) == (r2 % n_heads)) & ((c2 // n_heads) <= (r2 // n_heads))
    so = jnp.where(ok, so, NEG_INF)
    m_own = jnp.max(so, axis=1, keepdims=True)

    def max_body(n, m):
        return jnp.maximum(m, jnp.where(picked(n), ml_ref[n][:, 0:1] * scale, NEG_INF))

    m_all = lax.fori_loop(0, n_blk, max_body, m_own)
    p_own = jnp.exp(so - m_all)
    l_own = jnp.sum(p_own, axis=1, keepdims=True)
    acc_own = _dot(p_own.astype(BF16), vn_ref[...].astype(BF16))

    def sum_body(n, carry):
        l, acc = carry
        ml = ml_ref[n]
        w = jnp.where(picked(n), jnp.exp(ml[:, 0:1] * scale - m_all), 0.0)
        return l + w * ml[:, 1:2], acc + w * acc_ref[n]

    l, acc = lax.fori_loop(0, n_blk, sum_body, (l_own, acc_own))
    o_ref[...] = (acc / l).astype(o_ref.dtype)


def _sample_combine(q_rows, km, k_rows, v_rows, ml, acc, q_blk):
    n_seq, rows, head_dim = q_rows.shape
    n_blk, n_heads = km.shape[1], km.shape[2]
    seq_spec = pl.BlockSpec((None, rows, head_dim), lambda b: (b, 0, 0))
    part_spec = pl.BlockSpec((None, n_blk, rows, head_dim), lambda b: (b, 0, 0, 0))
    return pl.pallas_call(
        functools.partial(_sample_combine_kernel, n_heads=n_heads, q_blk=q_blk, scale=head_dim ** -0.5),
        out_shape=jax.ShapeDtypeStruct((n_seq, rows, head_dim), BF16),
        grid=(n_seq,),
        in_specs=[
            seq_spec,
            pl.BlockSpec((None, n_blk * n_heads, head_dim), lambda b: (b, 0, 0)),
            seq_spec, seq_spec, part_spec, part_spec,
        ],
        out_specs=seq_spec,
        scratch_shapes=[pltpu.VMEM(((n_blk + 1) * n_heads, head_dim), F32)],
        compiler_params=_params("parallel"),
        name="sample_select_combine",
    )(q_rows, km.reshape(n_seq, n_blk * n_heads, head_dim), k_rows, v_rows, ml, acc)


def _proj_ln_kernel(x_ref, w_ref, r_ref, g_ref, b_ref, y_ref, yb_ref, *, alpha):
    z = alpha * r_ref[...] + _dot(x_ref[...], w_ref[...])
    y = _layer_norm(z, g_ref[...], b_ref[...])
    y_ref[...] = y
    yb_ref[...] = y.astype(BF16)


def _proj_ln(x_bf, w_bf, resid, g, b, alpha):
    m, d_in = x_bf.shape
    d = w_bf.shape[1]
    tm = _row_tile(m, 256)
    row = lambda width: pl.BlockSpec((tm, width), lambda i: (i, 0))
    vec = pl.BlockSpec((1, d), lambda i: (0, 0))
    return pl.pallas_call(
        functools.partial(_proj_ln_kernel, alpha=alpha),
        out_shape=(jax.ShapeDtypeStruct((m, d), F32), jax.ShapeDtypeStruct((m, d), BF16)),
        grid=(m // tm,),
        in_specs=[row(d_in), pl.BlockSpec((d_in, d), lambda i: (0, 0)), row(d), vec, vec],
        out_specs=(row(d), row(d)),
        compiler_params=_params("parallel"),
        name="proj_residual_ln",
    )(x_bf, w_bf, resid, g.reshape(1, d), b.reshape(1, d))


def _ffn_ln_kernel(x_ref, wg_ref, wu_ref, wo_ref, r_ref, g_ref, b_ref, y_ref, yb_ref, acc_ref, *, alpha):
    f = pl.program_id(1)

    @pl.when(f == 0)
    def _():
        acc_ref[...] = jnp.zeros_like(acc_ref)

    x = x_ref[...]
    gate = _dot(x, wg_ref[...])
    up = _dot(x, wu_ref[...])
    hidden = (gate * _sigmoid(gate) * up).astype(BF16)
    acc_ref[...] += _dot(hidden, wo_ref[...])

    @pl.when(f == pl.num_programs(1) - 1)
    def _():
        y = _layer_norm(alpha * r_ref[...] + acc_ref[...], g_ref[...], b_ref[...])
        y_ref[...] = y
        yb_ref[...] = y.astype(BF16)


def _ffn_ln(x_bf, w_in_bf, w_out_bf, resid, g, b, alpha):
    m, d = x_bf.shape
    d_ff = w_out_bf.shape[0]
    tm = _row_tile(m, 512)
    tf = 512
    assert d_ff % tf == 0
    nf = d_ff // tf
    row = lambda: pl.BlockSpec((tm, d), lambda i, f: (i, 0))
    vec = pl.BlockSpec((1, d), lambda i, f: (0, 0))
    return pl.pallas_call(
        functools.partial(_ffn_ln_kernel, alpha=alpha),
        out_shape=(jax.ShapeDtypeStruct((m, d), F32), jax.ShapeDtypeStruct((m, d), BF16)),
        grid=(m // tm, nf),
        in_specs=[
            row(),
            pl.BlockSpec((d, tf), lambda i, f: (0, f)),
            pl.BlockSpec((d, tf), lambda i, f: (0, nf + f)),
            pl.BlockSpec((tf, d), lambda i, f: (f, 0)),
            row(), vec, vec,
        ],
        out_specs=(row(), row()),
        scratch_shapes=[pltpu.VMEM((tm, d), F32)],
        compiler_params=_params("parallel", "arbitrary"),
        name="swiglu_residual_ln",
    )(x_bf, w_in_bf, w_in_bf, w_out_bf, resid, g.reshape(1, d), b.reshape(1, d))


def _matmul_kernel(x_ref, w_ref, o_ref):
    o_ref[...] = _dot(x_ref[...], w_ref[...])


def _matmul(x_bf, w_bf):
    m, d = x_bf.shape
    n_out = w_bf.shape[1]
    tm = _row_tile(m, 1024)
    tn = 1024
    return pl.pallas_call(
        _matmul_kernel,
        out_shape=jax.ShapeDtypeStruct((m, n_out), F32),
        grid=(n_out // tn, m // tm),
        in_specs=[pl.BlockSpec((tm, d), lambda j, i: (i, 0)), pl.BlockSpec((d, tn), lambda j, i: (0, j))],
        out_specs=pl.BlockSpec((tm, tn), lambda j, i: (i, j)),
        compiler_params=_params("parallel", "parallel"),
        name="hgrn_in_proj",
    )(x_bf, w_bf)


def _hgrn_level_matrix(chunk):
    n_lev = int(math.log2(chunk))
    assert 2 ** n_lev == chunk
    t = np.arange(chunk)[:, None]
    s = np.arange(chunk)[None, :]
    mats = [s <= t]
    for lev in range(1, n_lev + 1):
        size = 2 ** lev
        mats.append(s < (t // size) * size + size // 2)
    return np.concatenate(mats, 0).astype(np.float32), n_lev


def _hgrn_kernel(q_ref, f_ref, i_ref, g_ref, lbl_ref, ng_ref, cm_ref, s0_ref, o_ref, s_ref, st_ref,
                 *, layer, n_valid, n_lev, n_heads):
    c = pl.program_id(1)
    chunk, d = q_ref.shape
    dk = d // n_heads

    @pl.when(c == 0)
    def _():
        for h in range(n_heads):
            st_ref[h] = s0_ref[h].T

    logits = lbl_ref[...]
    lrow = lax.broadcasted_iota(jnp.int32, logits.shape, 0)
    e = jnp.exp(logits - jnp.max(logits, axis=0, keepdims=True))
    sm = e / jnp.sum(e, axis=0, keepdims=True)
    lb = jnp.sum(jnp.where((lrow >= 1) & (lrow <= layer), sm, 0.0), axis=0, keepdims=True)

    row = lax.broadcasted_iota(jnp.int32, (chunk, d), 0)
    qraw = q_ref[...]
    q = qraw * _sigmoid(qraw)
    f = lb + (1.0 - lb) * _sigmoid(f_ref[...])
    k = 1.0 - f
    g = jnp.log(f)
    if n_valid < chunk:
        live = row < n_valid
        k = jnp.where(live, k, 0.0)
        g = jnp.where(live, g, 0.0)
    v = i_ref[...]
    vb = v.astype(BF16)
    graw = g_ref[...]
    out_gate = graw * _sigmoid(graw)

    cm = cm_ref[...]
    g1 = g.astype(BF16)
    r1 = g - g1.astype(F32)
    g2 = r1.astype(BF16)
    g3 = (r1 - g2.astype(F32)).astype(BF16)
    b_all = _dot(cm, g1) + _dot(cm, g2) + _dot(cm, g3)
    b = b_all[0:chunk]

    t_id = lax.broadcasted_iota(jnp.int32, (chunk, chunk), 0)
    s_id = lax.broadcasted_iota(jnp.int32, (chunk, chunk), 1)
    qb = q.astype(BF16)
    kb = k.astype(BF16)
    factors, pairs = [], []
    for lev in range(1, n_lev + 1):
        size = 2 ** lev
        half = size // 2
        decay = jnp.exp(-jnp.abs(b - b_all[lev * chunk:(lev + 1) * chunk]))
        factors.append((jnp.where((row % size) >= half, q, k) * decay).astype(BF16))
        pairs.append(((t_id // size) == (s_id // size)) & ((t_id % size) >= half) & ((s_id % size) < half))

    q_decayed = (q * jnp.exp(b)).astype(BF16)
    b_last = b[chunk - 1:chunk, :]
    k_decayed = (k * jnp.exp(b_last - b)).astype(BF16)
    state_decay = jnp.exp(b_last)
    ng = ng_ref[...]

    for h in range(n_heads):
        hs = slice(h * dk, (h + 1) * dk)
        a = jnp.where(t_id == s_id, _dot_nt(qb[:, hs], kb[:, hs]), 0.0)
        for x, pair in zip(factors, pairs):
            a = a + jnp.where(pair, _dot_nt(x[:, hs], x[:, hs]), 0.0)
        st = st_ref[h]
        o = _dot_nt(q_decayed[:, hs], st.astype(BF16)) + _dot(a.astype(BF16), vb[:, hs])
        st_ref[h] = st * state_decay[:, hs] + _dot(v[:, hs].T.astype(BF16), k_decayed[:, hs])
        o = o * lax.rsqrt(jnp.mean(o * o, axis=-1, keepdims=True) + RMS_EPS) * ng
        o_ref[:, hs] = (o * out_gate[:, hs]).astype(o_ref.dtype)

    @pl.when(c == pl.num_programs(1) - 1)
    def _():
        for h in range(n_heads):
            s_ref[h] = st_ref[h].T


def _hgrn_recurrence(y, s0, lb_logits, norm_g, layer, n_valid, chunk):
    batch, lp, d4 = y.shape
    d = d4 // 4
    _, n_heads, dk, dv = s0.shape
    assert n_heads * dk == d and dk == dv
    depth = lb_logits.shape[0]
    cm_np, n_lev = _hgrn_level_matrix(chunk)
    cm = jnp.asarray(cm_np, dtype=BF16)

    def quarter(qt):
        return pl.BlockSpec((None, chunk, d), lambda b, c: (b, c, qt))

    state_spec = pl.BlockSpec((None, n_heads, dk, dv), lambda b, c: (b, 0, 0, 0))
    return pl.pallas_call(
        functools.partial(_hgrn_kernel, layer=layer, n_valid=n_valid, n_lev=n_lev, n_heads=n_heads),
        out_shape=(jax.ShapeDtypeStruct((batch, lp, d), BF16),
                   jax.ShapeDtypeStruct((batch, n_heads, dk, dv), F32)),
        grid=(batch, lp // chunk),
        in_specs=[
            quarter(0), quarter(1), quarter(2), quarter(3),
            pl.BlockSpec((depth, d), lambda b, c: (0, 0)),
            pl.BlockSpec((1, dv), lambda b, c: (0, 0)),
            pl.BlockSpec(cm.shape, lambda b, c: (0, 0)),
            state_spec,
        ],
        out_specs=(pl.BlockSpec((None, chunk, d), lambda b, c: (b, c, 0)), state_spec),
        scratch_shapes=[pltpu.VMEM((n_heads, dv, dk), F32)],
        compiler_params=_params("parallel", "arbitrary"),
        name="hgrn_recurrence",
    )(y, y, y, y, lb_logits, norm_g.reshape(1, dv), cm, s0)


def _moba_layer(xp, xp_bf, xs, xs_bf, batch, n_seq, cache_k, cache_v, layer, page_table, w_qkv_bf, w_o_bf,
                ln_g, ln_b, alpha):
    d = xp.shape[-1]
    head_dim = d // N_HEADS
    seq = xp.shape[0] // batch
    n_new = xs.shape[0] // n_seq
    past = page_table.shape[1] * cache_k.shape[2]
    assert past % MOBA_BLOCK == 0 and n_new <= MOBA_BLOCK and cache_k.shape[3] == N_HEADS

    cos_p, sin_p = _rope_tables(jnp.arange(seq, dtype=jnp.int32), head_dim)
    qp, kp, vp = _qkv_rope(xp_bf, w_qkv_bf, jnp.tile(cos_p, (batch, 1)), jnp.tile(sin_p, (batch, 1)))
    op = _moba_prompt_attention(qp, kp, vp, batch)
    yp, yp_bf = _proj_ln(op, w_o_bf, xp, ln_g, ln_b, alpha)

    cos_s, sin_s = _rope_tables(past + jnp.arange(n_new, dtype=jnp.int32), head_dim)
    qs, ks, vs = _qkv_rope(xs_bf, w_qkv_bf, jnp.tile(cos_s, (n_seq, 1)), jnp.tile(sin_s, (n_seq, 1)))
    rows = lambda a: a.reshape(n_seq, n_new * N_HEADS, head_dim)
    km, ml, acc = _sample_stream(rows(qs), cache_k, cache_v, layer, page_table)
    os_ = _sample_combine(rows(qs), km, rows(ks), rows(vs), ml, acc, past // MOBA_BLOCK)
    ys, ys_bf = _proj_ln(os_.reshape(n_seq * n_new, d), w_o_bf, xs, ln_g, ln_b, alpha)

    kv_shape = lambda n, l: (n, l, N_HEADS, head_dim)
    return (yp, yp_bf, ys, ys_bf, kp.reshape(kv_shape(batch, seq)), vp.reshape(kv_shape(batch, seq)),
            ks.reshape(kv_shape(n_seq, n_new)), vs.reshape(kv_shape(n_seq, n_new)))


def _hgrn_group(x, x_bf, n_seq, s0, w_in_bf, w_o_bf, lb_logits, norm_g, layer, ln_g, ln_b, alpha):
    m, d = x.shape
    n_heads = d // HGRN_EXPAND
    seq = m // n_seq
    y = _matmul(x_bf, w_in_bf).reshape(n_seq, seq, 4 * d)
    if seq % HGRN_CHUNK == 0:
        chunk, lp = HGRN_CHUNK, seq
    else:
        chunk = lp = max(BF16_SUBLANES, int(2 ** math.ceil(math.log2(seq))))
        y = jnp.pad(y, ((0, 0), (0, lp - seq), (0, 0)))
    o, s = _hgrn_recurrence(y, s0, lb_logits, norm_g, layer, min(seq, chunk), chunk)
    o = o[:, :seq].reshape(m, d)
    y_out, y_out_bf = _proj_ln(o, w_o_bf, x, ln_g, ln_b, alpha)
    return y_out, y_out_bf, s


def kernel(x_prompt, x_sample, cache_k, cache_v, state_hgrn, page_table, attn_w_qkv, attn_w_o, hgrn_w_in,
           hgrn_lb_logits, hgrn_norm_g, hgrn_w_o, ffn_w_in, ffn_w_out, ln_mix_g, ln_mix_b, ln_ffn_g, ln_ffn_b):
    batch, seq, d = x_prompt.shape
    n_seq, n_new, _ = x_sample.shape
    depth = ffn_w_in.shape[0]
    alpha = (2.0 * depth) ** 0.25
    n_hgrn_heads = d // HGRN_EXPAND

    yp = x_prompt.reshape(batch * seq, d)
    ys = x_sample.reshape(n_seq * n_new, d)
    yp_bf, ys_bf = yp.astype(BF16), ys.astype(BF16)
    pk, pv, ps, sk, sv, ss = [], [], [], [], [], []
    for i in range(depth):
        if i % 2 == 0:
            a = i // 2
            yp, yp_bf, ys, ys_bf, kp, vp, kn, vn = _moba_layer(
                yp, yp_bf, ys, ys_bf, batch, n_seq, cache_k, cache_v, a, page_table,
                attn_w_qkv[a].astype(BF16), attn_w_o[a].astype(BF16), ln_mix_g[i], ln_mix_b[i], alpha)
            pk.append(kp)
            pv.append(vp)
            sk.append(kn)
            sv.append(vn)
        else:
            r = i // 2
            w_in_bf, w_o_bf = hgrn_w_in[r].astype(BF16), hgrn_w_o[r].astype(BF16)
            s0p = jnp.zeros((batch, n_hgrn_heads, HGRN_EXPAND, d // n_hgrn_heads), state_hgrn.dtype)
            yp, yp_bf, sp = _hgrn_group(yp, yp_bf, batch, s0p, w_in_bf, w_o_bf, hgrn_lb_logits, hgrn_norm_g[r],
                                        i, ln_mix_g[i], ln_mix_b[i], alpha)
            ys, ys_bf, sn = _hgrn_group(ys, ys_bf, n_seq, state_hgrn[r], w_in_bf, w_o_bf, hgrn_lb_logits,
                                        hgrn_norm_g[r], i, ln_mix_g[i], ln_mix_b[i], alpha)
            ps.append(sp)
            ss.append(sn)
        w_in_bf, w_out_bf = ffn_w_in[i].astype(BF16), ffn_w_out[i].astype(BF16)
        yp, yp_bf = _ffn_ln(yp_bf, w_in_bf, w_out_bf, yp, ln_ffn_g[i], ln_ffn_b[i], alpha)
        ys, ys_bf = _ffn_ln(ys_bf, w_in_bf, w_out_bf, ys, ln_ffn_g[i], ln_ffn_b[i], alpha)
    return (yp.reshape(batch, seq, d), ys.reshape(n_seq, n_new, d), jnp.stack(pk), jnp.stack(pv), jnp.stack(ps),
            jnp.stack(sk), jnp.stack(sv), jnp.stack(ss))
```

```python
import functools
import math

import numpy as np
import jax
import jax.numpy as jnp
from jax import lax
from jax.experimental import pallas as pl
from jax.experimental.pallas import tpu as pltpu

F32 = jnp.float32
BF16 = jnp.bfloat16
NEG_INF = float("-inf")

N_HEADS = 16
MOBA_BLOCK = 256
MOBA_TOP_K = 3
ROPE_THETA = 10000.0
HGRN_EXPAND = 128
HGRN_CHUNK = 128
LN_EPS = 1e-5
RMS_EPS = 1e-6

LANES = 128
SUBLANES = 8
BF16_SUBLANES = 16
VMEM_LIMIT_BYTES = 56 * 1024 * 1024

NT_DIMS = (((1,), (1,)), ((), ()))


def _params(*semantics):
    return pltpu.CompilerParams(dimension_semantics=semantics, vmem_limit_bytes=VMEM_LIMIT_BYTES)


def _row_tile(m, target):
    if m <= target:
        return m
    t = target
    while m % t:
        t //= 2
    return t


def _dot(a, b):
    return jnp.dot(a, b, preferred_element_type=F32)


def _dot_nt(a, b, precision=None):
    return lax.dot_general(a, b, NT_DIMS, precision=precision, preferred_element_type=F32)


def _sigmoid(x):
    return 1.0 / (1.0 + jnp.exp(-x))


def _layer_norm(z, g, b):
    mu = jnp.mean(z, axis=-1, keepdims=True)
    zc = z - mu
    var = jnp.mean(zc * zc, axis=-1, keepdims=True)
    return zc * lax.rsqrt(var + LN_EPS) * g + b


def _top_k_indices(gate, idf, n_ids, axis):
    picks = []
    for _ in range(MOBA_TOP_K):
        m = jnp.max(gate, axis=axis, keepdims=True)
        idx = jnp.min(jnp.where(gate == m, idf, float(n_ids)), axis=axis, keepdims=True)
        picks.append(jnp.where(m > NEG_INF, idx, -1.0))
        gate = jnp.where(idf == idx, NEG_INF, gate)
    return picks


CAST_BLOCK_BYTES = 4 * 1024 * 1024


def _cast_kernel(w_ref, o_ref):
    o_ref[...] = w_ref[...].astype(o_ref.dtype)


def _layer_bf16(w_stack, layer):
    _, rows, cols = w_stack.shape
    tr = _row_tile(rows, max(BF16_SUBLANES, 2 ** int(math.log2(CAST_BLOCK_BYTES // (cols * 4)))))
    return pl.pallas_call(
        _cast_kernel,
        out_shape=jax.ShapeDtypeStruct((rows, cols), BF16),
        grid=(rows // tr,),
        in_specs=[pl.BlockSpec((None, tr, cols), lambda i: (layer, i, 0))],
        out_specs=pl.BlockSpec((tr, cols), lambda i: (i, 0)),
        compiler_params=_params("parallel"),
        name="weight_to_bf16",
    )(w_stack)


def _qkv_rope_kernel(x_ref, wq_ref, wk_ref, wv_ref, cos_ref, sin_ref, q_ref, k_ref, v_ref, *, head_dim):
    x = x_ref[...]
    cos = cos_ref[...]
    sin = sin_ref[...]
    q = _dot(x, wq_ref[...])
    k = _dot(x, wk_ref[...])
    v_ref[...] = _dot(x, wv_ref[...])
    for h in range(q.shape[1] // head_dim):
        sl = slice(h * head_dim, (h + 1) * head_dim)
        qh = q[:, sl]
        kh = k[:, sl]
        q_ref[:, sl] = qh * cos + pltpu.roll(qh, head_dim // 2, axis=1) * sin
        k_ref[:, sl] = kh * cos + pltpu.roll(kh, head_dim // 2, axis=1) * sin


def _rope_tables(pos, head_dim):
    half = head_dim // 2
    inv = ROPE_THETA ** (-jnp.arange(half, dtype=F32) * (2.0 / head_dim))
    ang = pos.astype(F32)[:, None] * inv[None, :]
    cos, sin = jnp.cos(ang), jnp.sin(ang)
    return jnp.concatenate([cos, cos], -1), jnp.concatenate([-sin, sin], -1)


def _qkv_rope(x_bf, w_qkv_bf, cos, sin):
    m, d = x_bf.shape
    head_dim = d // N_HEADS
    tm = _row_tile(m, 1024)
    tn = 512
    nj = d // tn
    out = jax.ShapeDtypeStruct((m, d), F32)
    o_spec = pl.BlockSpec((tm, tn), lambda j, i: (i, j))
    return pl.pallas_call(
        functools.partial(_qkv_rope_kernel, head_dim=head_dim),
        out_shape=(out, out, out),
        grid=(nj, m // tm),
        in_specs=[
            pl.BlockSpec((tm, d), lambda j, i: (i, 0)),
            pl.BlockSpec((d, tn), lambda j, i: (0, j)),
            pl.BlockSpec((d, tn), lambda j, i: (0, nj + j)),
            pl.BlockSpec((d, tn), lambda j, i: (0, 2 * nj + j)),
            pl.BlockSpec((tm, head_dim), lambda j, i: (i, 0)),
            pl.BlockSpec((tm, head_dim), lambda j, i: (i, 0)),
        ],
        out_specs=(o_spec, o_spec, o_spec),
        compiler_params=_params("parallel", "parallel"),
        name="qkv_rope",
    )(x_bf, w_qkv_bf, w_qkv_bf, w_qkv_bf, cos, sin)


def _moba_prompt_kernel(q_ref, k_ref, v_ref, o_ref, kb_ref, vt_ref, km_ref, p_ref,
                        *, n_blocks, heads, kv_unroll, scale):
    qi = pl.program_id(2)
    blk = MOBA_BLOCK
    head_dim = q_ref.shape[1] // heads
    exp2_scale = scale * math.log2(math.e)

    @pl.when(qi == 0)
    def _():
        for h in range(heads):
            cols = slice(h * head_dim, (h + 1) * head_dim)
            for n in range(n_blocks):
                rows = slice(n * blk, (n + 1) * blk)
                kf = k_ref[rows, cols]
                km_ref[h, n:n + 1, :] = jnp.sum(kf, axis=0, keepdims=True) * (1.0 / blk)
                kb_ref[h, n] = kf.astype(BF16)
                vt_ref[h, n] = v_ref[rows, cols].T.astype(BF16)

    key_id = lax.broadcasted_iota(jnp.int32, (blk, blk), 0)
    qry_id = lax.broadcasted_iota(jnp.int32, (blk, blk), 1)
    blk_id = lax.broadcasted_iota(jnp.int32, (n_blocks, blk), 0)
    qtbs, picks, init = [], [], []
    for h in range(heads):
        qt = q_ref[:, h * head_dim:(h + 1) * head_dim].T
        gate = jnp.dot(km_ref[h], qt, precision=lax.Precision.HIGHEST, preferred_element_type=F32)
        gate = jnp.where(blk_id < qi, gate, NEG_INF)
        picks.append(_top_k_indices(gate, blk_id.astype(F32), n_blocks, axis=0))
        qtb = qt.astype(BF16)
        qtbs.append(qtb)
        st = jnp.where(key_id <= qry_id, _dot(kb_ref[h, qi], qtb), NEG_INF)
        m0 = jnp.max(st, axis=0, keepdims=True)
        p = jnp.exp2((st - m0) * exp2_scale)
        p_ref[h, 0] = p.astype(BF16)
        for u in range(1, kv_unroll):
            p_ref[h, u] = jnp.zeros((blk, blk), BF16)
        init.append((m0, jnp.sum(p, axis=0, keepdims=True), jnp.zeros((head_dim, blk), F32)))

    def pending_values(h, pending):
        return functools.reduce(lambda a, b: a + b, [_dot(vt_ref[h, nc], p_ref[h, u]) for u, nc in enumerate(pending)])

    def body(it, carry):
        state, pending = carry
        ids = [it * kv_unroll + u for u in range(kv_unroll)]
        clamped = [jnp.minimum(n, n_blocks - 1) for n in ids]
        scores = [[_dot(kb_ref[h, nc], qtbs[h]) for nc in clamped] for h in range(heads)]
        flushed = [pending_values(h, pending) for h in range(heads)]
        out = []
        for h in range(heads):
            m, l, acc = state[h]
            s1, s2, s3 = picks[h]
            m_new = m
            blocks = []
            for n, sn in zip(ids, scores[h]):
                nf = jnp.asarray(n, F32)
                picked = (s1 == nf) | (s2 == nf) | (s3 == nf)
                m_new = jnp.maximum(m_new, jnp.where(picked, jnp.max(sn, axis=0, keepdims=True), NEG_INF))
                blocks.append((picked, sn))
            alpha = jnp.exp2((m - m_new) * exp2_scale)
            l = alpha * l
            for u, (picked, sn) in enumerate(blocks):
                shift = jnp.where(picked, m_new, jnp.inf)
                pn = jnp.exp2((sn - shift) * exp2_scale)
                l = l + jnp.sum(pn, axis=0, keepdims=True)
                p_ref[h, u] = pn.astype(BF16)
            out.append((m_new, l, alpha * (acc + flushed[h])))
        return tuple(out), tuple(clamped)

    own = tuple(qi for _ in range(kv_unroll))
    final, pending = lax.fori_loop(0, (qi + kv_unroll - 1) // kv_unroll, body, (tuple(init), own))
    for h in range(heads):
        _, l, acc = final[h]
        o = (acc + pending_values(h, pending)) / l
        o_ref[:, h * head_dim:(h + 1) * head_dim] = o.T.astype(o_ref.dtype)


def _moba_prompt_attention(q, k, v, batch, heads=2, kv_unroll=2):
    m, d = q.shape
    seq = m // batch
    head_dim = d // N_HEADS
    assert seq % MOBA_BLOCK == 0 and seq // MOBA_BLOCK >= MOBA_TOP_K and N_HEADS % heads == 0
    n_blocks = seq // MOBA_BLOCK
    width = heads * head_dim
    kv_spec = pl.BlockSpec((seq, width), lambda b, h, i: (b, h))
    qo_spec = pl.BlockSpec((MOBA_BLOCK, width), lambda b, h, i: (b * n_blocks + i, h))
    return pl.pallas_call(
        functools.partial(_moba_prompt_kernel, n_blocks=n_blocks, heads=heads, kv_unroll=kv_unroll,
                          scale=head_dim ** -0.5),
        out_shape=jax.ShapeDtypeStruct((m, d), BF16),
        grid=(batch, N_HEADS // heads, n_blocks),
        in_specs=[qo_spec, kv_spec, kv_spec],
        out_specs=qo_spec,
        scratch_shapes=[
            pltpu.VMEM((heads, n_blocks, MOBA_BLOCK, head_dim), BF16),
            pltpu.VMEM((heads, n_blocks, head_dim, MOBA_BLOCK), BF16),
            pltpu.VMEM((heads, n_blocks, head_dim), F32),
            pltpu.VMEM((heads, kv_unroll, MOBA_BLOCK, MOBA_BLOCK), BF16),
        ],
        compiler_params=_params("parallel", "parallel", "arbitrary"),
        name="moba_prompt_attention",
    )(q, k, v)


def _sample_stream_kernel(pt_ref, q_ref, bias_ref, *refs, blocks_per_step, pages_per_block, scale):
    del pt_ref
    n_pages = blocks_per_step * pages_per_block
    k_refs, v_refs = refs[:n_pages], refs[n_pages:2 * n_pages]
    km_ref, m_ref, l_ref, acc_ref = refs[2 * n_pages:]
    page, n_heads, head_dim = k_refs[0].shape
    exp2_scale = scale * math.log2(math.e)
    qb = q_ref[...].astype(BF16)
    bias = bias_ref[...]
    step = pl.program_id(1)
    lane = lax.broadcasted_iota(jnp.int32, m_ref.shape, 1)

    @pl.when(step == 0)
    def _():
        m_ref[...] = jnp.zeros_like(m_ref)
        l_ref[...] = jnp.zeros_like(l_ref)

    for j in range(blocks_per_step):
        pages = range(j * pages_per_block, (j + 1) * pages_per_block)
        k_sum = None
        scores = []
        for pg in pages:
            kf = k_refs[pg][...]
            k_sum = jnp.sum(kf, axis=0) if k_sum is None else k_sum + jnp.sum(kf, axis=0)
            scores.append(_dot_nt(qb, kf.reshape(page * n_heads, head_dim).astype(BF16)) + bias)
        km_ref[j] = k_sum * (1.0 / (page * pages_per_block))
        m = functools.reduce(jnp.maximum, [jnp.max(s, axis=1, keepdims=True) for s in scores])
        l = None
        acc = None
        for pg, s in zip(pages, scores):
            p = jnp.exp2((s - m) * exp2_scale)
            pv = _dot(p.astype(BF16), v_refs[pg][...].reshape(page * n_heads, head_dim).astype(BF16))
            ps = jnp.sum(p, axis=1, keepdims=True)
            l, acc = (ps, pv) if l is None else (l + ps, acc + pv)
        acc_ref[j] = acc
        mine = lane == step * blocks_per_step + j
        m_ref[...] = jnp.where(mine, m, m_ref[...])
        l_ref[...] = jnp.where(mine, l, l_ref[...])


def _sample_stream(q_rows, cache_k, cache_v, layer, page_table, blocks_per_step=4):
    n_seq, rows, head_dim = q_rows.shape
    page, n_heads = cache_k.shape[2], cache_k.shape[3]
    assert MOBA_BLOCK % page == 0
    ppb = MOBA_BLOCK // page
    n_blk = page_table.shape[1] // ppb
    assert n_blk % blocks_per_step == 0 and n_blk <= LANES
    n_pages = blocks_per_step * ppb
    head_of_row = np.arange(rows)[:, None] % n_heads
    head_of_key = np.arange(page * n_heads)[None, :] % n_heads
    bias = jnp.asarray(np.where(head_of_row == head_of_key, 0.0, -np.inf), dtype=F32)

    def page_spec(j):
        return pl.BlockSpec((None, None, page, n_heads, head_dim),
                            lambda b, n, pt: (layer, pt[b, n * n_pages + j], 0, 0, 0))

    part = jax.ShapeDtypeStruct((n_seq, n_blk, rows, head_dim), F32)
    part_spec = pl.BlockSpec((None, blocks_per_step, rows, head_dim), lambda b, n, pt: (b, n, 0, 0))
    stat = jax.ShapeDtypeStruct((n_seq, rows, LANES), F32)
    stat_spec = pl.BlockSpec((None, rows, LANES), lambda b, n, pt: (b, 0, 0))
    page_specs = [page_spec(j) for j in range(n_pages)]
    return pl.pallas_call(
        functools.partial(_sample_stream_kernel, blocks_per_step=blocks_per_step, pages_per_block=ppb,
                          scale=head_dim ** -0.5),
        out_shape=(jax.ShapeDtypeStruct((n_seq, n_blk, n_heads, head_dim), F32), stat, stat, part),
        grid_spec=pltpu.PrefetchScalarGridSpec(
            num_scalar_prefetch=1,
            grid=(n_seq, n_blk // blocks_per_step),
            in_specs=[
                pl.BlockSpec((None, rows, head_dim), lambda b, n, pt: (b, 0, 0)),
                pl.BlockSpec(bias.shape, lambda b, n, pt: (0, 0)),
            ] + page_specs + page_specs,
            out_specs=(
                pl.BlockSpec((None, blocks_per_step, n_heads, head_dim), lambda b, n, pt: (b, n, 0, 0)),
                stat_spec, stat_spec, part_spec,
            ),
        ),
        compiler_params=_params("parallel", "arbitrary"),
        name="sample_cache_stream",
    )(page_table, q_rows, bias, *([cache_k] * n_pages), *([cache_v] * n_pages))


def _sample_combine_kernel(q_ref, km_ref, kn_ref, vn_ref, m_ref, l_ref, acc_ref, o_ref, kmx_ref,
                           *, n_heads, q_blk, scale):
    n_blk = acc_ref.shape[0]
    n_past = n_blk * n_heads
    qf = q_ref[...]
    kn = kn_ref[...]
    rows, head_dim = qf.shape

    kmx_ref[0:n_past, :] = km_ref[...]
    own_sum = jnp.sum(kn.reshape(rows // n_heads, n_heads, head_dim), axis=0)
    kmx_ref[n_past:n_past + n_heads, :] = own_sum * (1.0 / MOBA_BLOCK)
    gate = _dot_nt(qf, kmx_ref[...], precision=lax.Precision.HIGHEST)
    col = lax.broadcasted_iota(jnp.int32, gate.shape, 1)
    row = lax.broadcasted_iota(jnp.int32, gate.shape, 0)
    valid = ((col % n_heads) == (row % n_heads)) & ((col // n_heads) < q_blk)
    gate = jnp.where(valid, gate, NEG_INF)
    picks = _top_k_indices(gate, col.astype(F32), gate.shape[1], axis=1)
    blk_lane = lax.broadcasted_iota(jnp.int32, m_ref.shape, 1).astype(F32)
    picked = functools.reduce(jnp.logical_or, [blk_lane == jnp.floor(s * (1.0 / n_heads)) for s in picks])

    so = _dot_nt(qf.astype(BF16), kn.astype(BF16)) * scale
    r2 = lax.broadcasted_iota(jnp.int32, so.shape, 0)
    c2 = lax.broadcasted_iota(jnp.int32, so.shape, 1)
    ok = ((c2 % n_heads) == (r2 % n_heads)) & ((c2 // n_heads) <= (r2 // n_heads))
    so = jnp.where(ok, so, NEG_INF)
    m_own = jnp.max(so, axis=1, keepdims=True)

    m_blk = m_ref[...] * scale
    m_all = jnp.maximum(m_own, jnp.max(jnp.where(picked, m_blk, NEG_INF), axis=1, keepdims=True))
    w = jnp.where(picked, jnp.exp(m_blk - m_all), 0.0)
    p_own = jnp.exp(so - m_all)
    l = jnp.sum(p_own, axis=1, keepdims=True) + jnp.sum(w * l_ref[...], axis=1, keepdims=True)
    acc = _dot(p_own.astype(BF16), vn_ref[...].astype(BF16))
    for n in range(n_blk):
        acc = acc + w[:, n:n + 1] * acc_ref[n]
    o_ref[...] = (acc / l).astype(o_ref.dtype)


def _sample_combine(q_rows, km, k_rows, v_rows, m_blk, l_blk, acc, q_blk):
    n_seq, rows, head_dim = q_rows.shape
    n_blk, n_heads = km.shape[1], km.shape[2]
    seq_spec = pl.BlockSpec((None, rows, head_dim), lambda b: (b, 0, 0))
    stat_spec = pl.BlockSpec((None, rows, m_blk.shape[2]), lambda b: (b, 0, 0))
    part_spec = pl.BlockSpec((None, n_blk, rows, head_dim), lambda b: (b, 0, 0, 0))
    return pl.pallas_call(
        functools.partial(_sample_combine_kernel, n_heads=n_heads, q_blk=q_blk, scale=head_dim ** -0.5),
        out_shape=jax.ShapeDtypeStruct((n_seq, rows, head_dim), BF16),
        grid=(n_seq,),
        in_specs=[
            seq_spec,
            pl.BlockSpec((None, n_blk * n_heads, head_dim), lambda b: (b, 0, 0)),
            seq_spec, seq_spec, stat_spec, stat_spec, part_spec,
        ],
        out_specs=seq_spec,
        scratch_shapes=[pltpu.VMEM(((n_blk + 1) * n_heads, head_dim), F32)],
        compiler_params=_params("parallel"),
        name="sample_select_combine",
    )(q_rows, km.reshape(n_seq, n_blk * n_heads, head_dim), k_rows, v_rows, m_blk, l_blk, acc)


def _proj_ln_kernel(x_ref, w_ref, r_ref, g_ref, b_ref, y_ref, yb_ref, *, alpha):
    z = alpha * r_ref[...] + _dot(x_ref[...], w_ref[...])
    y = _layer_norm(z, g_ref[...], b_ref[...])
    y_ref[...] = y
    yb_ref[...] = y.astype(BF16)


def _proj_ln(x_bf, w_bf, resid, g, b, alpha):
    m, d_in = x_bf.shape
    d = w_bf.shape[1]
    tm = _row_tile(m, 512)
    row = lambda width: pl.BlockSpec((tm, width), lambda i: (i, 0))
    vec = pl.BlockSpec((1, d), lambda i: (0, 0))
    return pl.pallas_call(
        functools.partial(_proj_ln_kernel, alpha=alpha),
        out_shape=(jax.ShapeDtypeStruct((m, d), F32), jax.ShapeDtypeStruct((m, d), BF16)),
        grid=(m // tm,),
        in_specs=[row(d_in), pl.BlockSpec((d_in, d), lambda i: (0, 0)), row(d), vec, vec],
        out_specs=(row(d), row(d)),
        compiler_params=_params("parallel"),
        name="proj_residual_ln",
    )(x_bf, w_bf, resid, g.reshape(1, d), b.reshape(1, d))


def _ffn_ln_kernel(x_ref, wg_ref, wu_ref, wo_ref, r_ref, g_ref, b_ref, y_ref, yb_ref, acc_ref, *, alpha):
    f = pl.program_id(1)

    @pl.when(f == 0)
    def _():
        acc_ref[...] = jnp.zeros_like(acc_ref)

    x = x_ref[...]
    gate = _dot(x, wg_ref[...])
    up = _dot(x, wu_ref[...])
    hidden = (gate * _sigmoid(gate) * up).astype(BF16)
    acc_ref[...] += _dot(hidden, wo_ref[...])

    @pl.when(f == pl.num_programs(1) - 1)
    def _():
        y = _layer_norm(alpha * r_ref[...] + acc_ref[...], g_ref[...], b_ref[...])
        y_ref[...] = y
        yb_ref[...] = y.astype(BF16)


def _ffn_ln(x_bf, w_in_bf, w_out_bf, resid, g, b, alpha):
    m, d = x_bf.shape
    d_ff = w_out_bf.shape[0]
    tm = _row_tile(m, 512)
    tf = 512
    assert d_ff % tf == 0
    nf = d_ff // tf
    row = lambda: pl.BlockSpec((tm, d), lambda i, f: (i, 0))
    vec = pl.BlockSpec((1, d), lambda i, f: (0, 0))
    return pl.pallas_call(
        functools.partial(_ffn_ln_kernel, alpha=alpha),
        out_shape=(jax.ShapeDtypeStruct((m, d), F32), jax.ShapeDtypeStruct((m, d), BF16)),
        grid=(m // tm, nf),
        in_specs=[
            row(),
            pl.BlockSpec((d, tf), lambda i, f: (0, f)),
            pl.BlockSpec((d, tf), lambda i, f: (0, nf + f)),
            pl.BlockSpec((tf, d), lambda i, f: (f, 0)),
            row(), vec, vec,
        ],
        out_specs=(row(), row()),
        scratch_shapes=[pltpu.VMEM((tm, d), F32)],
        compiler_params=_params("parallel", "arbitrary"),
        name="swiglu_residual_ln",
    )(x_bf, w_in_bf, w_in_bf, w_out_bf, resid, g.reshape(1, d), b.reshape(1, d))


def _matmul_kernel(x_ref, w_ref, o_ref):
    o_ref[...] = _dot(x_ref[...], w_ref[...])


def _matmul(x_bf, w_bf):
    m, d = x_bf.shape
    n_out = w_bf.shape[1]
    tm = _row_tile(m, 1024)
    tn = 1024
    return pl.pallas_call(
        _matmul_kernel,
        out_shape=jax.ShapeDtypeStruct((m, n_out), F32),
        grid=(n_out // tn, m // tm),
        in_specs=[pl.BlockSpec((tm, d), lambda j, i: (i, 0)), pl.BlockSpec((d, tn), lambda j, i: (0, j))],
        out_specs=pl.BlockSpec((tm, tn), lambda j, i: (i, j)),
        compiler_params=_params("parallel", "parallel"),
        name="hgrn_in_proj",
    )(x_bf, w_bf)


def _hgrn_level_matrix(chunk):
    n_lev = int(math.log2(chunk))
    assert 2 ** n_lev == chunk
    t = np.arange(chunk)[:, None]
    s = np.arange(chunk)[None, :]
    mats = [s <= t]
    for lev in range(1, n_lev + 1):
        size = 2 ** lev
        mats.append(s < (t // size) * size + size // 2)
    return np.concatenate(mats, 0).astype(np.float32), n_lev


def _hgrn_kernel(q_ref, f_ref, i_ref, g_ref, lbl_ref, ng_ref, cm_ref, s0_ref, o_ref, s_ref, st_ref,
                 *, layer, n_valid, n_lev, n_heads):
    c = pl.program_id(1)
    chunk, d = q_ref.shape
    dk = d // n_heads

    @pl.when(c == 0)
    def _():
        for h in range(n_heads):
            st_ref[h] = s0_ref[h].T

    logits = lbl_ref[...]
    lrow = lax.broadcasted_iota(jnp.int32, logits.shape, 0)
    e = jnp.exp(logits - jnp.max(logits, axis=0, keepdims=True))
    sm = e / jnp.sum(e, axis=0, keepdims=True)
    lb = jnp.sum(jnp.where((lrow >= 1) & (lrow <= layer), sm, 0.0), axis=0, keepdims=True)

    row = lax.broadcasted_iota(jnp.int32, (chunk, d), 0)
    qraw = q_ref[...]
    q = qraw * _sigmoid(qraw)
    f = lb + (1.0 - lb) * _sigmoid(f_ref[...])
    k = 1.0 - f
    g = jnp.log(f)
    if n_valid < chunk:
        live = row < n_valid
        k = jnp.where(live, k, 0.0)
        g = jnp.where(live, g, 0.0)
    v = i_ref[...]
    vb = v.astype(BF16)
    graw = g_ref[...]
    out_gate = graw * _sigmoid(graw)

    cm = cm_ref[...]
    g1 = g.astype(BF16)
    r1 = g - g1.astype(F32)
    g2 = r1.astype(BF16)
    g3 = (r1 - g2.astype(F32)).astype(BF16)
    b_all = _dot(cm, g1) + _dot(cm, g2) + _dot(cm, g3)
    b = b_all[0:chunk]

    t_id = lax.broadcasted_iota(jnp.int32, (chunk, chunk), 0)
    s_id = lax.broadcasted_iota(jnp.int32, (chunk, chunk), 1)
    qb = q.astype(BF16)
    kb = k.astype(BF16)
    factors, pairs = [], []
    for lev in range(1, n_lev + 1):
        size = 2 ** lev
        half = size // 2
        decay = jnp.exp(-jnp.abs(b - b_all[lev * chunk:(lev + 1) * chunk]))
        factors.append((jnp.where((row % size) >= half, q, k) * decay).astype(BF16))
        pairs.append(((t_id // size) == (s_id // size)) & ((t_id % size) >= half) & ((s_id % size) < half))

    q_decayed = (q * jnp.exp(b)).astype(BF16)
    b_last = b[chunk - 1:chunk, :]
    k_decayed = (k * jnp.exp(b_last - b)).astype(BF16)
    state_decay = jnp.exp(b_last)
    ng = ng_ref[...]

    for h in range(n_heads):
        hs = slice(h * dk, (h + 1) * dk)
        a = jnp.where(t_id == s_id, _dot_nt(qb[:, hs], kb[:, hs]), 0.0)
        for x, pair in zip(factors, pairs):
            a = a + jnp.where(pair, _dot_nt(x[:, hs], x[:, hs]), 0.0)
        st = st_ref[h]
        o = _dot_nt(q_decayed[:, hs], st.astype(BF16)) + _dot(a.astype(BF16), vb[:, hs])
        st_ref[h] = st * state_decay[:, hs] + _dot(v[:, hs].T.astype(BF16), k_decayed[:, hs])
        o = o * lax.rsqrt(jnp.mean(o * o, axis=-1, keepdims=True) + RMS_EPS) * ng
        o_ref[:, hs] = (o * out_gate[:, hs]).astype(o_ref.dtype)

    @pl.when(c == pl.num_programs(1) - 1)
    def _():
        for h in range(n_heads):
            s_ref[h] = st_ref[h].T


def _hgrn_recurrence(y, s0, lb_logits, norm_g, layer, n_valid, chunk):
    batch, lp, d4 = y.shape
    d = d4 // 4
    _, n_heads, dk, dv = s0.shape
    assert n_heads * dk == d and dk == dv
    depth = lb_logits.shape[0]
    cm_np, n_lev = _hgrn_level_matrix(chunk)
    cm = jnp.asarray(cm_np, dtype=BF16)

    def quarter(qt):
        return pl.BlockSpec((None, chunk, d), lambda b, c: (b, c, qt))

    state_spec = pl.BlockSpec((None, n_heads, dk, dv), lambda b, c: (b, 0, 0, 0))
    return pl.pallas_call(
        functools.partial(_hgrn_kernel, layer=layer, n_valid=n_valid, n_lev=n_lev, n_heads=n_heads),
        out_shape=(jax.ShapeDtypeStruct((batch, lp, d), BF16),
                   jax.ShapeDtypeStruct((batch, n_heads, dk, dv), F32)),
        grid=(batch, lp // chunk),
        in_specs=[
            quarter(0), quarter(1), quarter(2), quarter(3),
            pl.BlockSpec((depth, d), lambda b, c: (0, 0)),
            pl.BlockSpec((1, dv), lambda b, c: (0, 0)),
            pl.BlockSpec(cm.shape, lambda b, c: (0, 0)),
            state_spec,
        ],
        out_specs=(pl.BlockSpec((None, chunk, d), lambda b, c: (b, c, 0)), state_spec),
        scratch_shapes=[pltpu.VMEM((n_heads, dv, dk), F32)],
        compiler_params=_params("parallel", "arbitrary"),
        name="hgrn_recurrence",
    )(y, y, y, y, lb_logits, norm_g.reshape(1, dv), cm, s0)


def _moba_layer(xp, xp_bf, xs, xs_bf, batch, n_seq, cache_k, cache_v, layer, page_table, w_qkv_bf, w_o_bf,
                ln_g, ln_b, alpha):
    d = xp.shape[-1]
    head_dim = d // N_HEADS
    seq = xp.shape[0] // batch
    n_new = xs.shape[0] // n_seq
    past = page_table.shape[1] * cache_k.shape[2]
    assert past % MOBA_BLOCK == 0 and n_new <= MOBA_BLOCK and cache_k.shape[3] == N_HEADS

    cos_p, sin_p = _rope_tables(jnp.arange(seq, dtype=jnp.int32), head_dim)
    qp, kp, vp = _qkv_rope(xp_bf, w_qkv_bf, jnp.tile(cos_p, (batch, 1)), jnp.tile(sin_p, (batch, 1)))
    op = _moba_prompt_attention(qp, kp, vp, batch)
    yp, yp_bf = _proj_ln(op, w_o_bf, xp, ln_g, ln_b, alpha)

    cos_s, sin_s = _rope_tables(past + jnp.arange(n_new, dtype=jnp.int32), head_dim)
    qs, ks, vs = _qkv_rope(xs_bf, w_qkv_bf, jnp.tile(cos_s, (n_seq, 1)), jnp.tile(sin_s, (n_seq, 1)))
    rows = lambda a: a.reshape(n_seq, n_new * N_HEADS, head_dim)
    km, m_blk, l_blk, acc = _sample_stream(rows(qs), cache_k, cache_v, layer, page_table)
    os_ = _sample_combine(rows(qs), km, rows(ks), rows(vs), m_blk, l_blk, acc, past // MOBA_BLOCK)
    ys, ys_bf = _proj_ln(os_.reshape(n_seq * n_new, d), w_o_bf, xs, ln_g, ln_b, alpha)

    kv_shape = lambda n, l: (n, l, N_HEADS, head_dim)
    return (yp, yp_bf, ys, ys_bf, kp.reshape(kv_shape(batch, seq)), vp.reshape(kv_shape(batch, seq)),
            ks.reshape(kv_shape(n_seq, n_new)), vs.reshape(kv_shape(n_seq, n_new)))


def _hgrn_group(x, x_bf, n_seq, s0, w_in_bf, w_o_bf, lb_logits, norm_g, layer, ln_g, ln_b, alpha):
    m, d = x.shape
    n_heads = d // HGRN_EXPAND
    seq = m // n_seq
    y = _matmul(x_bf, w_in_bf).reshape(n_seq, seq, 4 * d)
    if seq % HGRN_CHUNK == 0:
        chunk, lp = HGRN_CHUNK, seq
    else:
        chunk = lp = max(BF16_SUBLANES, int(2 ** math.ceil(math.log2(seq))))
        y = jnp.pad(y, ((0, 0), (0, lp - seq), (0, 0)))
    o, s = _hgrn_recurrence(y, s0, lb_logits, norm_g, layer, min(seq, chunk), chunk)
    o = o[:, :seq].reshape(m, d)
    y_out, y_out_bf = _proj_ln(o, w_o_bf, x, ln_g, ln_b, alpha)
    return y_out, y_out_bf, s


def kernel(x_prompt, x_sample, cache_k, cache_v, state_hgrn, page_table, attn_w_qkv, attn_w_o, hgrn_w_in,
           hgrn_lb_logits, hgrn_norm_g, hgrn_w_o, ffn_w_in, ffn_w_out, ln_mix_g, ln_mix_b, ln_ffn_g, ln_ffn_b):
    batch, seq, d = x_prompt.shape
    n_seq, n_new, _ = x_sample.shape
    depth = ffn_w_in.shape[0]
    alpha = (2.0 * depth) ** 0.25
    n_hgrn_heads = d // HGRN_EXPAND

    yp = x_prompt.reshape(batch * seq, d)
    ys = x_sample.reshape(n_seq * n_new, d)
    yp_bf, ys_bf = yp.astype(BF16), ys.astype(BF16)
    pk, pv, ps, sk, sv, ss = [], [], [], [], [], []
    for i in range(depth):
        if i % 2 == 0:
            a = i // 2
            yp, yp_bf, ys, ys_bf, kp, vp, kn, vn = _moba_layer(
                yp, yp_bf, ys, ys_bf, batch, n_seq, cache_k, cache_v, a, page_table,
                _layer_bf16(attn_w_qkv, a), _layer_bf16(attn_w_o, a), ln_mix_g[i], ln_mix_b[i], alpha)
            pk.append(kp)
            pv.append(vp)
            sk.append(kn)
            sv.append(vn)
        else:
            r = i // 2
            w_in_bf, w_o_bf = _layer_bf16(hgrn_w_in, r), _layer_bf16(hgrn_w_o, r)
            s0p = jnp.zeros((batch, n_hgrn_heads, HGRN_EXPAND, d // n_hgrn_heads), state_hgrn.dtype)
            yp, yp_bf, sp = _hgrn_group(yp, yp_bf, batch, s0p, w_in_bf, w_o_bf, hgrn_lb_logits, hgrn_norm_g[r],
                                        i, ln_mix_g[i], ln_mix_b[i], alpha)
            ys, ys_bf, sn = _hgrn_group(ys, ys_bf, n_seq, state_hgrn[r], w_in_bf, w_o_bf, hgrn_lb_logits,
                                        hgrn_norm_g[r], i, ln_mix_g[i], ln_mix_b[i], alpha)
            ps.append(sp)
            ss.append(sn)
        w_in_bf, w_out_bf = _layer_bf16(ffn_w_in, i), _layer_bf16(ffn_w_out, i)
        yp, yp_bf = _ffn_ln(yp_bf, w_in_bf, w_out_bf, yp, ln_ffn_g[i], ln_ffn_b[i], alpha)
        ys, ys_bf = _ffn_ln(ys_bf, w_in_bf, w_out_bf, ys, ln_ffn_g[i], ln_ffn_b[i], alpha)
    return (yp.reshape(batch, seq, d), ys.reshape(n_seq, n_new, d), jnp.stack(pk), jnp.stack(pv), jnp.stack(ps),
            jnp.stack(sk), jnp.stack(sv), jnp.stack(ss))
```

```python
import functools
import math

import numpy as np
import jax
import jax.numpy as jnp
from jax import lax
from jax.experimental import pallas as pl
from jax.experimental.pallas import tpu as pltpu

F32 = jnp.float32
BF16 = jnp.bfloat16
NEG_INF = float("-inf")

N_HEADS = 16
MOBA_BLOCK = 256
MOBA_TOP_K = 3
ROPE_THETA = 10000.0
HGRN_EXPAND = 128
HGRN_CHUNK = 128
LN_EPS = 1e-5
RMS_EPS = 1e-6

LANES = 128
SUBLANES = 8
BF16_SUBLANES = 16
VMEM_LIMIT_BYTES = 56 * 1024 * 1024

NT_DIMS = (((1,), (1,)), ((), ()))


def _params(*semantics):
    return pltpu.CompilerParams(dimension_semantics=semantics, vmem_limit_bytes=VMEM_LIMIT_BYTES)


def _row_tile(m, target):
    if m <= target:
        return m
    t = target
    while m % t:
        t //= 2
    return t


def _dot(a, b):
    return jnp.dot(a, b, preferred_element_type=F32)


def _dot_nt(a, b, precision=None):
    return lax.dot_general(a, b, NT_DIMS, precision=precision, preferred_element_type=F32)


def _sigmoid(x):
    return 1.0 / (1.0 + jnp.exp(-x))


def _layer_norm(z, g, b):
    mu = jnp.mean(z, axis=-1, keepdims=True)
    zc = z - mu
    var = jnp.mean(zc * zc, axis=-1, keepdims=True)
    return zc * lax.rsqrt(var + LN_EPS) * g + b


def _top_k_indices(gate, idf, n_ids, axis):
    picks = []
    for _ in range(MOBA_TOP_K):
        m = jnp.max(gate, axis=axis, keepdims=True)
        idx = jnp.min(jnp.where(gate == m, idf, float(n_ids)), axis=axis, keepdims=True)
        picks.append(jnp.where(m > NEG_INF, idx, -1.0))
        gate = jnp.where(idf == idx, NEG_INF, gate)
    return picks


CAST_BLOCK_BYTES = 4 * 1024 * 1024


def _cast_kernel(w_ref, o_ref):
    o_ref[...] = w_ref[...].astype(o_ref.dtype)


def _layer_bf16(w_stack, layer):
    _, rows, cols = w_stack.shape
    tr = _row_tile(rows, max(BF16_SUBLANES, 2 ** int(math.log2(CAST_BLOCK_BYTES // (cols * 4)))))
    return pl.pallas_call(
        _cast_kernel,
        out_shape=jax.ShapeDtypeStruct((rows, cols), BF16),
        grid=(rows // tr,),
        in_specs=[pl.BlockSpec((None, tr, cols), lambda i: (layer, i, 0))],
        out_specs=pl.BlockSpec((tr, cols), lambda i: (i, 0)),
        compiler_params=_params("parallel"),
        name="weight_to_bf16",
    )(w_stack)


def _qkv_rope_kernel(x_ref, wq_ref, wk_ref, wv_ref, cos_ref, sin_ref, q_ref, k_ref, v_ref, *, head_dim):
    x = x_ref[...]
    cos = cos_ref[...]
    sin = sin_ref[...]
    q = _dot(x, wq_ref[...])
    k = _dot(x, wk_ref[...])
    v_ref[...] = _dot(x, wv_ref[...])
    for h in range(q.shape[1] // head_dim):
        sl = slice(h * head_dim, (h + 1) * head_dim)
        qh = q[:, sl]
        kh = k[:, sl]
        q_ref[:, sl] = qh * cos + pltpu.roll(qh, head_dim // 2, axis=1) * sin
        k_ref[:, sl] = kh * cos + pltpu.roll(kh, head_dim // 2, axis=1) * sin


def _rope_tables(pos, head_dim):
    half = head_dim // 2
    inv = ROPE_THETA ** (-jnp.arange(half, dtype=F32) * (2.0 / head_dim))
    ang = pos.astype(F32)[:, None] * inv[None, :]
    cos, sin = jnp.cos(ang), jnp.sin(ang)
    return jnp.concatenate([cos, cos], -1), jnp.concatenate([-sin, sin], -1)


def _qkv_rope(x_bf, w_qkv_bf, cos, sin):
    m, d = x_bf.shape
    head_dim = d // N_HEADS
    tm = _row_tile(m, 1024)
    tn = 512
    nj = d // tn
    out = jax.ShapeDtypeStruct((m, d), F32)
    o_spec = pl.BlockSpec((tm, tn), lambda j, i: (i, j))
    return pl.pallas_call(
        functools.partial(_qkv_rope_kernel, head_dim=head_dim),
        out_shape=(out, out, out),
        grid=(nj, m // tm),
        in_specs=[
            pl.BlockSpec((tm, d), lambda j, i: (i, 0)),
            pl.BlockSpec((d, tn), lambda j, i: (0, j)),
            pl.BlockSpec((d, tn), lambda j, i: (0, nj + j)),
            pl.BlockSpec((d, tn), lambda j, i: (0, 2 * nj + j)),
            pl.BlockSpec((tm, head_dim), lambda j, i: (i, 0)),
            pl.BlockSpec((tm, head_dim), lambda j, i: (i, 0)),
        ],
        out_specs=(o_spec, o_spec, o_spec),
        compiler_params=_params("parallel", "parallel"),
        name="qkv_rope",
    )(x_bf, w_qkv_bf, w_qkv_bf, w_qkv_bf, cos, sin)


def _moba_prompt_kernel(q_ref, k_ref, v_ref, o_ref, kb_ref, vt_ref, km_ref, p_ref,
                        *, n_blocks, heads, kv_unroll, scale):
    qi = pl.program_id(2)
    blk = MOBA_BLOCK
    head_dim = q_ref.shape[1] // heads
    exp2_scale = scale * math.log2(math.e)

    @pl.when(qi == 0)
    def _():
        for h in range(heads):
            cols = slice(h * head_dim, (h + 1) * head_dim)
            for n in range(n_blocks):
                rows = slice(n * blk, (n + 1) * blk)
                kf = k_ref[rows, cols]
                km_ref[h, n:n + 1, :] = jnp.sum(kf, axis=0, keepdims=True) * (1.0 / blk)
                kb_ref[h, n] = kf.astype(BF16)
                vt_ref[h, n] = v_ref[rows, cols].T.astype(BF16)

    key_id = lax.broadcasted_iota(jnp.int32, (blk, blk), 0)
    qry_id = lax.broadcasted_iota(jnp.int32, (blk, blk), 1)
    blk_id = lax.broadcasted_iota(jnp.int32, (n_blocks, blk), 0)
    qtbs, picks, init = [], [], []
    for h in range(heads):
        qt = q_ref[:, h * head_dim:(h + 1) * head_dim].T
        gate = jnp.dot(km_ref[h], qt, precision=lax.Precision.HIGHEST, preferred_element_type=F32)
        gate = jnp.where(blk_id < qi, gate, NEG_INF)
        picks.append(_top_k_indices(gate, blk_id.astype(F32), n_blocks, axis=0))
        qtb = qt.astype(BF16)
        qtbs.append(qtb)
        st = jnp.where(key_id <= qry_id, _dot(kb_ref[h, qi], qtb), NEG_INF)
        m0 = jnp.max(st, axis=0, keepdims=True)
        p = jnp.exp2((st - m0) * exp2_scale)
        p_ref[h, 0] = p.astype(BF16)
        for u in range(1, kv_unroll):
            p_ref[h, u] = jnp.zeros((blk, blk), BF16)
        init.append((m0, jnp.sum(p, axis=0, keepdims=True), jnp.zeros((head_dim, blk), F32)))

    def pending_values(h, pending):
        return functools.reduce(lambda a, b: a + b, [_dot(vt_ref[h, nc], p_ref[h, u]) for u, nc in enumerate(pending)])

    def body(it, carry):
        state, pending = carry
        ids = [it * kv_unroll + u for u in range(kv_unroll)]
        clamped = [jnp.minimum(n, n_blocks - 1) for n in ids]
        scores = [[_dot(kb_ref[h, nc], qtbs[h]) for nc in clamped] for h in range(heads)]
        flushed = [pending_values(h, pending) for h in range(heads)]
        out = []
        for h in range(heads):
            m, l, acc = state[h]
            s1, s2, s3 = picks[h]
            m_new = m
            blocks = []
            for n, sn in zip(ids, scores[h]):
                nf = jnp.asarray(n, F32)
                picked = (s1 == nf) | (s2 == nf) | (s3 == nf)
                m_new = jnp.maximum(m_new, jnp.where(picked, jnp.max(sn, axis=0, keepdims=True), NEG_INF))
                blocks.append((picked, sn))
            alpha = jnp.exp2((m - m_new) * exp2_scale)
            l = alpha * l
            for u, (picked, sn) in enumerate(blocks):
                shift = jnp.where(picked, m_new, jnp.inf)
                pn = jnp.exp2((sn - shift) * exp2_scale)
                l = l + jnp.sum(pn, axis=0, keepdims=True)
                p_ref[h, u] = pn.astype(BF16)
            out.append((m_new, l, alpha * (acc + flushed[h])))
        return tuple(out), tuple(clamped)

    own = tuple(qi for _ in range(kv_unroll))
    final, pending = lax.fori_loop(0, (qi + kv_unroll - 1) // kv_unroll, body, (tuple(init), own))
    for h in range(heads):
        _, l, acc = final[h]
        o = (acc + pending_values(h, pending)) / l
        o_ref[:, h * head_dim:(h + 1) * head_dim] = o.T.astype(o_ref.dtype)


def _moba_prompt_attention(q, k, v, batch, heads=4, kv_unroll=2):
    m, d = q.shape
    seq = m // batch
    head_dim = d // N_HEADS
    assert seq % MOBA_BLOCK == 0 and seq // MOBA_BLOCK >= MOBA_TOP_K and N_HEADS % heads == 0
    n_blocks = seq // MOBA_BLOCK
    width = heads * head_dim
    kv_spec = pl.BlockSpec((seq, width), lambda b, h, i: (b, h))
    qo_spec = pl.BlockSpec((MOBA_BLOCK, width), lambda b, h, i: (b * n_blocks + i, h))
    return pl.pallas_call(
        functools.partial(_moba_prompt_kernel, n_blocks=n_blocks, heads=heads, kv_unroll=kv_unroll,
                          scale=head_dim ** -0.5),
        out_shape=jax.ShapeDtypeStruct((m, d), BF16),
        grid=(batch, N_HEADS // heads, n_blocks),
        in_specs=[qo_spec, kv_spec, kv_spec],
        out_specs=qo_spec,
        scratch_shapes=[
            pltpu.VMEM((heads, n_blocks, MOBA_BLOCK, head_dim), BF16),
            pltpu.VMEM((heads, n_blocks, head_dim, MOBA_BLOCK), BF16),
            pltpu.VMEM((heads, n_blocks, head_dim), F32),
            pltpu.VMEM((heads, kv_unroll, MOBA_BLOCK, MOBA_BLOCK), BF16),
        ],
        compiler_params=_params("parallel", "parallel", "arbitrary"),
        name="moba_prompt_attention",
    )(q, k, v)


def _sample_stream_kernel(pt_ref, q_ref, bias_ref, *refs, blocks_per_step, pages_per_block, scale):
    del pt_ref
    n_pages = blocks_per_step * pages_per_block
    k_refs, v_refs = refs[:n_pages], refs[n_pages:2 * n_pages]
    km_ref, m_ref, l_ref, acc_ref = refs[2 * n_pages:]
    page, n_heads, head_dim = k_refs[0].shape
    exp2_scale = scale * math.log2(math.e)
    qb = q_ref[...].astype(BF16)
    bias = bias_ref[...]
    step = pl.program_id(1)
    lane = lax.broadcasted_iota(jnp.int32, m_ref.shape, 1)

    @pl.when(step == 0)
    def _():
        m_ref[...] = jnp.zeros_like(m_ref)
        l_ref[...] = jnp.zeros_like(l_ref)

    for j in range(blocks_per_step):
        pages = range(j * pages_per_block, (j + 1) * pages_per_block)
        k_sum = None
        scores = []
        for pg in pages:
            kf = k_refs[pg][...]
            k_sum = jnp.sum(kf, axis=0) if k_sum is None else k_sum + jnp.sum(kf, axis=0)
            scores.append(_dot_nt(qb, kf.reshape(page * n_heads, head_dim).astype(BF16)) + bias)
        km_ref[j] = k_sum * (1.0 / (page * pages_per_block))
        m = functools.reduce(jnp.maximum, [jnp.max(s, axis=1, keepdims=True) for s in scores])
        l = None
        acc = None
        for pg, s in zip(pages, scores):
            p = jnp.exp2((s - m) * exp2_scale)
            pv = _dot(p.astype(BF16), v_refs[pg][...].reshape(page * n_heads, head_dim).astype(BF16))
            ps = jnp.sum(p, axis=1, keepdims=True)
            l, acc = (ps, pv) if l is None else (l + ps, acc + pv)
        acc_ref[j] = acc
        mine = lane == step * blocks_per_step + j
        m_ref[...] = jnp.where(mine, m, m_ref[...])
        l_ref[...] = jnp.where(mine, l, l_ref[...])


def _sample_stream(q_rows, cache_k, cache_v, layer, page_table, blocks_per_step=4):
    n_seq, rows, head_dim = q_rows.shape
    page, n_heads = cache_k.shape[2], cache_k.shape[3]
    assert MOBA_BLOCK % page == 0
    ppb = MOBA_BLOCK // page
    n_blk = page_table.shape[1] // ppb
    assert n_blk % blocks_per_step == 0 and n_blk <= LANES
    n_pages = blocks_per_step * ppb
    head_of_row = np.arange(rows)[:, None] % n_heads
    head_of_key = np.arange(page * n_heads)[None, :] % n_heads
    bias = jnp.asarray(np.where(head_of_row == head_of_key, 0.0, -np.inf), dtype=F32)

    def page_spec(j):
        return pl.BlockSpec((None, None, page, n_heads, head_dim),
                            lambda b, n, pt: (layer, pt[b, n * n_pages + j], 0, 0, 0))

    part = jax.ShapeDtypeStruct((n_seq, n_blk, rows, head_dim), F32)
    part_spec = pl.BlockSpec((None, blocks_per_step, rows, head_dim), lambda b, n, pt: (b, n, 0, 0))
    stat = jax.ShapeDtypeStruct((n_seq, rows, LANES), F32)
    stat_spec = pl.BlockSpec((None, rows, LANES), lambda b, n, pt: (b, 0, 0))
    page_specs = [page_spec(j) for j in range(n_pages)]
    return pl.pallas_call(
        functools.partial(_sample_stream_kernel, blocks_per_step=blocks_per_step, pages_per_block=ppb,
                          scale=head_dim ** -0.5),
        out_shape=(jax.ShapeDtypeStruct((n_seq, n_blk, n_heads, head_dim), F32), stat, stat, part),
        grid_spec=pltpu.PrefetchScalarGridSpec(
            num_scalar_prefetch=1,
            grid=(n_seq, n_blk // blocks_per_step),
            in_specs=[
                pl.BlockSpec((None, rows, head_dim), lambda b, n, pt: (b, 0, 0)),
                pl.BlockSpec(bias.shape, lambda b, n, pt: (0, 0)),
            ] + page_specs + page_specs,
            out_specs=(
                pl.BlockSpec((None, blocks_per_step, n_heads, head_dim), lambda b, n, pt: (b, n, 0, 0)),
                stat_spec, stat_spec, part_spec,
            ),
        ),
        compiler_params=_params("parallel", "arbitrary"),
        name="sample_cache_stream",
    )(page_table, q_rows, bias, *([cache_k] * n_pages), *([cache_v] * n_pages))


def _sample_combine_kernel(q_ref, km_ref, kn_ref, vn_ref, m_ref, l_ref, acc_ref, o_ref, kmx_ref,
                           *, n_heads, q_blk, scale):
    n_blk = acc_ref.shape[0]
    n_past = n_blk * n_heads
    qf = q_ref[...]
    kn = kn_ref[...]
    rows, head_dim = qf.shape

    kmx_ref[0:n_past, :] = km_ref[...]
    own_sum = jnp.sum(kn.reshape(rows // n_heads, n_heads, head_dim), axis=0)
    kmx_ref[n_past:n_past + n_heads, :] = own_sum * (1.0 / MOBA_BLOCK)
    gate = _dot_nt(qf, kmx_ref[...], precision=lax.Precision.HIGHEST)
    col = lax.broadcasted_iota(jnp.int32, gate.shape, 1)
    row = lax.broadcasted_iota(jnp.int32, gate.shape, 0)
    valid = ((col % n_heads) == (row % n_heads)) & ((col // n_heads) < q_blk)
    gate = jnp.where(valid, gate, NEG_INF)
    picks = _top_k_indices(gate, col.astype(F32), gate.shape[1], axis=1)
    blk_lane = lax.broadcasted_iota(jnp.int32, m_ref.shape, 1).astype(F32)
    picked = functools.reduce(jnp.logical_or, [blk_lane == jnp.floor(s * (1.0 / n_heads)) for s in picks])

    so = _dot_nt(qf.astype(BF16), kn.astype(BF16)) * scale
    r2 = lax.broadcasted_iota(jnp.int32, so.shape, 0)
    c2 = lax.broadcasted_iota(jnp.int32, so.shape, 1)
    ok = ((c2 % n_heads) == (r2 % n_heads)) & ((c2 // n_heads) <= (r2 // n_heads))
    so = jnp.where(ok, so, NEG_INF)
    m_own = jnp.max(so, axis=1, keepdims=True)

    m_blk = m_ref[...] * scale
    m_all = jnp.maximum(m_own, jnp.max(jnp.where(picked, m_blk, NEG_INF), axis=1, keepdims=True))
    w = jnp.where(picked, jnp.exp(m_blk - m_all), 0.0)
    p_own = jnp.exp(so - m_all)
    l = jnp.sum(p_own, axis=1, keepdims=True) + jnp.sum(w * l_ref[...], axis=1, keepdims=True)
    acc = _dot(p_own.astype(BF16), vn_ref[...].astype(BF16))
    for n in range(n_blk):
        acc = acc + w[:, n:n + 1] * acc_ref[n]
    o_ref[...] = (acc / l).astype(o_ref.dtype)


def _sample_combine(q_rows, km, k_rows, v_rows, m_blk, l_blk, acc, q_blk):
    n_seq, rows, head_dim = q_rows.shape
    n_blk, n_heads = km.shape[1], km.shape[2]
    seq_spec = pl.BlockSpec((None, rows, head_dim), lambda b: (b, 0, 0))
    stat_spec = pl.BlockSpec((None, rows, m_blk.shape[2]), lambda b: (b, 0, 0))
    part_spec = pl.BlockSpec((None, n_blk, rows, head_dim), lambda b: (b, 0, 0, 0))
    return pl.pallas_call(
        functools.partial(_sample_combine_kernel, n_heads=n_heads, q_blk=q_blk, scale=head_dim ** -0.5),
        out_shape=jax.ShapeDtypeStruct((n_seq, rows, head_dim), BF16),
        grid=(n_seq,),
        in_specs=[
            seq_spec,
            pl.BlockSpec((None, n_blk * n_heads, head_dim), lambda b: (b, 0, 0)),
            seq_spec, seq_spec, stat_spec, stat_spec, part_spec,
        ],
        out_specs=seq_spec,
        scratch_shapes=[pltpu.VMEM(((n_blk + 1) * n_heads, head_dim), F32)],
        compiler_params=_params("parallel"),
        name="sample_select_combine",
    )(q_rows, km.reshape(n_seq, n_blk * n_heads, head_dim), k_rows, v_rows, m_blk, l_blk, acc)


def _proj_ln_kernel(x_ref, w_ref, r_ref, g_ref, b_ref, y_ref, yb_ref, *, alpha):
    tm = x_ref.shape[0]
    n_parts = 2 if tm % (2 * BF16_SUBLANES) == 0 else 1
    for part in range(n_parts):
        rows = slice(part * tm // n_parts, (part + 1) * tm // n_parts)
        z = alpha * r_ref[rows, :] + _dot(x_ref[rows, :], w_ref[...])
        y = _layer_norm(z, g_ref[...], b_ref[...])
        y_ref[rows, :] = y
        yb_ref[rows, :] = y.astype(BF16)


def _proj_ln(x_bf, w_bf, resid, g, b, alpha):
    m, d_in = x_bf.shape
    d = w_bf.shape[1]
    tm = _row_tile(m, 512)
    row = lambda width: pl.BlockSpec((tm, width), lambda i: (i, 0))
    vec = pl.BlockSpec((1, d), lambda i: (0, 0))
    return pl.pallas_call(
        functools.partial(_proj_ln_kernel, alpha=alpha),
        out_shape=(jax.ShapeDtypeStruct((m, d), F32), jax.ShapeDtypeStruct((m, d), BF16)),
        grid=(m // tm,),
        in_specs=[row(d_in), pl.BlockSpec((d_in, d), lambda i: (0, 0)), row(d), vec, vec],
        out_specs=(row(d), row(d)),
        compiler_params=_params("parallel"),
        name="proj_residual_ln",
    )(x_bf, w_bf, resid, g.reshape(1, d), b.reshape(1, d))


def _ffn_ln_kernel(x_ref, wg_ref, wu_ref, wo_ref, r_ref, g_ref, b_ref, y_ref, yb_ref, acc_ref, *, alpha):
    f = pl.program_id(1)

    @pl.when(f == 0)
    def _():
        acc_ref[...] = jnp.zeros_like(acc_ref)

    x = x_ref[...]
    gate = _dot(x, wg_ref[...])
    up = _dot(x, wu_ref[...])
    hidden = (gate * _sigmoid(gate) * up).astype(BF16)
    acc_ref[...] += _dot(hidden, wo_ref[...])

    @pl.when(f == pl.num_programs(1) - 1)
    def _():
        y = _layer_norm(alpha * r_ref[...] + acc_ref[...], g_ref[...], b_ref[...])
        y_ref[...] = y
        yb_ref[...] = y.astype(BF16)


def _ffn_ln(x_bf, w_in_bf, w_out_bf, resid, g, b, alpha):
    m, d = x_bf.shape
    d_ff = w_out_bf.shape[0]
    tm = _row_tile(m, 512)
    tf = 512
    assert d_ff % tf == 0
    nf = d_ff // tf
    row = lambda: pl.BlockSpec((tm, d), lambda i, f: (i, 0))
    vec = pl.BlockSpec((1, d), lambda i, f: (0, 0))
    return pl.pallas_call(
        functools.partial(_ffn_ln_kernel, alpha=alpha),
        out_shape=(jax.ShapeDtypeStruct((m, d), F32), jax.ShapeDtypeStruct((m, d), BF16)),
        grid=(m // tm, nf),
        in_specs=[
            row(),
            pl.BlockSpec((d, tf), lambda i, f: (0, f)),
            pl.BlockSpec((d, tf), lambda i, f: (0, nf + f)),
            pl.BlockSpec((tf, d), lambda i, f: (f, 0)),
            row(), vec, vec,
        ],
        out_specs=(row(), row()),
        scratch_shapes=[pltpu.VMEM((tm, d), F32)],
        compiler_params=_params("parallel", "arbitrary"),
        name="swiglu_residual_ln",
    )(x_bf, w_in_bf, w_in_bf, w_out_bf, resid, g.reshape(1, d), b.reshape(1, d))


def _matmul_kernel(x_ref, w_ref, o_ref):
    o_ref[...] = _dot(x_ref[...], w_ref[...])


def _matmul(x_bf, w_bf):
    m, d = x_bf.shape
    n_out = w_bf.shape[1]
    tm = _row_tile(m, 1024)
    tn = 1024
    return pl.pallas_call(
        _matmul_kernel,
        out_shape=jax.ShapeDtypeStruct((m, n_out), F32),
        grid=(n_out // tn, m // tm),
        in_specs=[pl.BlockSpec((tm, d), lambda j, i: (i, 0)), pl.BlockSpec((d, tn), lambda j, i: (0, j))],
        out_specs=pl.BlockSpec((tm, tn), lambda j, i: (i, j)),
        compiler_params=_params("parallel", "parallel"),
        name="hgrn_in_proj",
    )(x_bf, w_bf)


def _hgrn_level_matrix(chunk):
    n_lev = int(math.log2(chunk))
    assert 2 ** n_lev == chunk
    t = np.arange(chunk)[:, None]
    s = np.arange(chunk)[None, :]
    mats = [s <= t]
    for lev in range(1, n_lev + 1):
        size = 2 ** lev
        mats.append(s < (t // size) * size + size // 2)
    return np.concatenate(mats, 0).astype(np.float32), n_lev


def _hgrn_kernel(q_ref, f_ref, i_ref, g_ref, lbl_ref, ng_ref, cm_ref, s0_ref, o_ref, s_ref, st_ref,
                 *, layer, n_valid, n_lev, n_heads):
    c = pl.program_id(1)
    chunk, d = q_ref.shape
    dk = d // n_heads

    @pl.when(c == 0)
    def _():
        for h in range(n_heads):
            st_ref[h] = s0_ref[h].T

    logits = lbl_ref[...]
    lrow = lax.broadcasted_iota(jnp.int32, logits.shape, 0)
    e = jnp.exp(logits - jnp.max(logits, axis=0, keepdims=True))
    sm = e / jnp.sum(e, axis=0, keepdims=True)
    lb = jnp.sum(jnp.where((lrow >= 1) & (lrow <= layer), sm, 0.0), axis=0, keepdims=True)

    row = lax.broadcasted_iota(jnp.int32, (chunk, d), 0)
    qraw = q_ref[...]
    q = qraw * _sigmoid(qraw)
    f = lb + (1.0 - lb) * _sigmoid(f_ref[...])
    k = 1.0 - f
    g = jnp.log(f)
    if n_valid < chunk:
        live = row < n_valid
        k = jnp.where(live, k, 0.0)
        g = jnp.where(live, g, 0.0)
    v = i_ref[...]
    vb = v.astype(BF16)
    graw = g_ref[...]
    out_gate = graw * _sigmoid(graw)

    cm = cm_ref[...]
    g1 = g.astype(BF16)
    r1 = g - g1.astype(F32)
    g2 = r1.astype(BF16)
    g3 = (r1 - g2.astype(F32)).astype(BF16)
    b_all = _dot(cm, g1) + _dot(cm, g2) + _dot(cm, g3)
    b = b_all[0:chunk]

    t_id = lax.broadcasted_iota(jnp.int32, (chunk, chunk), 0)
    s_id = lax.broadcasted_iota(jnp.int32, (chunk, chunk), 1)
    qb = q.astype(BF16)
    kb = k.astype(BF16)
    factors, pairs = [], []
    for lev in range(1, n_lev + 1):
        size = 2 ** lev
        half = size // 2
        decay = jnp.exp(-jnp.abs(b - b_all[lev * chunk:(lev + 1) * chunk]))
        factors.append((jnp.where((row % size) >= half, q, k) * decay).astype(BF16))
        pairs.append(((t_id // size) == (s_id // size)) & ((t_id % size) >= half) & ((s_id % size) < half))

    q_decayed = (q * jnp.exp(b)).astype(BF16)
    b_last = b[chunk - 1:chunk, :]
    k_decayed = (k * jnp.exp(b_last - b)).astype(BF16)
    state_decay = jnp.exp(b_last)
    ng = ng_ref[...]

    for h in range(n_heads):
        hs = slice(h * dk, (h + 1) * dk)
        a = jnp.where(t_id == s_id, _dot_nt(qb[:, hs], kb[:, hs]), 0.0)
        for x, pair in zip(factors, pairs):
            a = a + jnp.where(pair, _dot_nt(x[:, hs], x[:, hs]), 0.0)
        st = st_ref[h]
        o = _dot_nt(q_decayed[:, hs], st.astype(BF16)) + _dot(a.astype(BF16), vb[:, hs])
        st_ref[h] = st * state_decay[:, hs] + _dot(v[:, hs].T.astype(BF16), k_decayed[:, hs])
        o = o * lax.rsqrt(jnp.mean(o * o, axis=-1, keepdims=True) + RMS_EPS) * ng
        o_ref[:, hs] = (o * out_gate[:, hs]).astype(o_ref.dtype)

    @pl.when(c == pl.num_programs(1) - 1)
    def _():
        for h in range(n_heads):
            s_ref[h] = st_ref[h].T


def _hgrn_recurrence(y, s0, lb_logits, norm_g, layer, n_valid, chunk):
    batch, lp, d4 = y.shape
    d = d4 // 4
    _, n_heads, dk, dv = s0.shape
    assert n_heads * dk == d and dk == dv
    depth = lb_logits.shape[0]
    cm_np, n_lev = _hgrn_level_matrix(chunk)
    cm = jnp.asarray(cm_np, dtype=BF16)

    def quarter(qt):
        return pl.BlockSpec((None, chunk, d), lambda b, c: (b, c, qt))

    state_spec = pl.BlockSpec((None, n_heads, dk, dv), lambda b, c: (b, 0, 0, 0))
    return pl.pallas_call(
        functools.partial(_hgrn_kernel, layer=layer, n_valid=n_valid, n_lev=n_lev, n_heads=n_heads),
        out_shape=(jax.ShapeDtypeStruct((batch, lp, d), BF16),
                   jax.ShapeDtypeStruct((batch, n_heads, dk, dv), F32)),
        grid=(batch, lp // chunk),
        in_specs=[
            quarter(0), quarter(1), quarter(2), quarter(3),
            pl.BlockSpec((depth, d), lambda b, c: (0, 0)),
            pl.BlockSpec((1, dv), lambda b, c: (0, 0)),
            pl.BlockSpec(cm.shape, lambda b, c: (0, 0)),
            state_spec,
        ],
        out_specs=(pl.BlockSpec((None, chunk, d), lambda b, c: (b, c, 0)), state_spec),
        scratch_shapes=[pltpu.VMEM((n_heads, dv, dk), F32)],
        compiler_params=_params("parallel", "arbitrary"),
        name="hgrn_recurrence",
    )(y, y, y, y, lb_logits, norm_g.reshape(1, dv), cm, s0)


def _moba_layer(xp, xp_bf, xs, xs_bf, batch, n_seq, cache_k, cache_v, layer, page_table, w_qkv_bf, w_o_bf,
                ln_g, ln_b, alpha):
    d = xp.shape[-1]
    head_dim = d // N_HEADS
    seq = xp.shape[0] // batch
    n_new = xs.shape[0] // n_seq
    past = page_table.shape[1] * cache_k.shape[2]
    assert past % MOBA_BLOCK == 0 and n_new <= MOBA_BLOCK and cache_k.shape[3] == N_HEADS

    cos_p, sin_p = _rope_tables(jnp.arange(seq, dtype=jnp.int32), head_dim)
    qp, kp, vp = _qkv_rope(xp_bf, w_qkv_bf, jnp.tile(cos_p, (batch, 1)), jnp.tile(sin_p, (batch, 1)))
    op = _moba_prompt_attention(qp, kp, vp, batch)
    yp, yp_bf = _proj_ln(op, w_o_bf, xp, ln_g, ln_b, alpha)

    cos_s, sin_s = _rope_tables(past + jnp.arange(n_new, dtype=jnp.int32), head_dim)
    qs, ks, vs = _qkv_rope(xs_bf, w_qkv_bf, jnp.tile(cos_s, (n_seq, 1)), jnp.tile(sin_s, (n_seq, 1)))
    rows = lambda a: a.reshape(n_seq, n_new * N_HEADS, head_dim)
    km, m_blk, l_blk, acc = _sample_stream(rows(qs), cache_k, cache_v, layer, page_table)
    os_ = _sample_combine(rows(qs), km, rows(ks), rows(vs), m_blk, l_blk, acc, past // MOBA_BLOCK)
    ys, ys_bf = _proj_ln(os_.reshape(n_seq * n_new, d), w_o_bf, xs, ln_g, ln_b, alpha)

    kv_shape = lambda n, l: (n, l, N_HEADS, head_dim)
    return (yp, yp_bf, ys, ys_bf, kp.reshape(kv_shape(batch, seq)), vp.reshape(kv_shape(batch, seq)),
            ks.reshape(kv_shape(n_seq, n_new)), vs.reshape(kv_shape(n_seq, n_new)))


def _hgrn_group(x, x_bf, n_seq, s0, w_in_bf, w_o_bf, lb_logits, norm_g, layer, ln_g, ln_b, alpha):
    m, d = x.shape
    n_heads = d // HGRN_EXPAND
    seq = m // n_seq
    y = _matmul(x_bf, w_in_bf).reshape(n_seq, seq, 4 * d)
    if seq % HGRN_CHUNK == 0:
        chunk, lp = HGRN_CHUNK, seq
    else:
        chunk = lp = max(BF16_SUBLANES, int(2 ** math.ceil(math.log2(seq))))
        y = jnp.pad(y, ((0, 0), (0, lp - seq), (0, 0)))
    o, s = _hgrn_recurrence(y, s0, lb_logits, norm_g, layer, min(seq, chunk), chunk)
    o = o[:, :seq].reshape(m, d)
    y_out, y_out_bf = _proj_ln(o, w_o_bf, x, ln_g, ln_b, alpha)
    return y_out, y_out_bf, s


def kernel(x_prompt, x_sample, cache_k, cache_v, state_hgrn, page_table, attn_w_qkv, attn_w_o, hgrn_w_in,
           hgrn_lb_logits, hgrn_norm_g, hgrn_w_o, ffn_w_in, ffn_w_out, ln_mix_g, ln_mix_b, ln_ffn_g, ln_ffn_b):
    batch, seq, d = x_prompt.shape
    n_seq, n_new, _ = x_sample.shape
    depth = ffn_w_in.shape[0]
    alpha = (2.0 * depth) ** 0.25
    n_hgrn_heads = d // HGRN_EXPAND

    yp = x_prompt.reshape(batch * seq, d)
    ys = x_sample.reshape(n_seq * n_new, d)
    yp_bf, ys_bf = yp.astype(BF16), ys.astype(BF16)
    pk, pv, ps, sk, sv, ss = [], [], [], [], [], []
    for i in range(depth):
        if i % 2 == 0:
            a = i // 2
            yp, yp_bf, ys, ys_bf, kp, vp, kn, vn = _moba_layer(
                yp, yp_bf, ys, ys_bf, batch, n_seq, cache_k, cache_v, a, page_table,
                _layer_bf16(attn_w_qkv, a), _layer_bf16(attn_w_o, a), ln_mix_g[i], ln_mix_b[i], alpha)
            pk.append(kp)
            pv.append(vp)
            sk.append(kn)
            sv.append(vn)
        else:
            r = i // 2
            w_in_bf, w_o_bf = _layer_bf16(hgrn_w_in, r), _layer_bf16(hgrn_w_o, r)
            s0p = jnp.zeros((batch, n_hgrn_heads, HGRN_EXPAND, d // n_hgrn_heads), state_hgrn.dtype)
            yp, yp_bf, sp = _hgrn_group(yp, yp_bf, batch, s0p, w_in_bf, w_o_bf, hgrn_lb_logits, hgrn_norm_g[r],
                                        i, ln_mix_g[i], ln_mix_b[i], alpha)
            ys, ys_bf, sn = _hgrn_group(ys, ys_bf, n_seq, state_hgrn[r], w_in_bf, w_o_bf, hgrn_lb_logits,
                                        hgrn_norm_g[r], i, ln_mix_g[i], ln_mix_b[i], alpha)
            ps.append(sp)
            ss.append(sn)
        w_in_bf, w_out_bf = _layer_bf16(ffn_w_in, i), _layer_bf16(ffn_w_out, i)
        yp, yp_bf = _ffn_ln(yp_bf, w_in_bf, w_out_bf, yp, ln_ffn_g[i], ln_ffn_b[i], alpha)
        ys, ys_bf = _ffn_ln(ys_bf, w_in_bf, w_out_bf, ys, ln_ffn_g[i], ln_ffn_b[i], alpha)
    return (yp.reshape(batch, seq, d), ys.reshape(n_seq, n_new, d), jnp.stack(pk), jnp.stack(pv), jnp.stack(ps),
            jnp.stack(sk), jnp.stack(sv), jnp.stack(ss))
```

```python
import functools
import math

import numpy as np
import jax
import jax.numpy as jnp
from jax import lax
from jax.experimental import pallas as pl
from jax.experimental.pallas import tpu as pltpu

F32 = jnp.float32
BF16 = jnp.bfloat16
NEG_INF = float("-inf")

N_HEADS = 16
MOBA_BLOCK = 256
MOBA_TOP_K = 3
ROPE_THETA = 10000.0
HGRN_EXPAND = 128
HGRN_CHUNK = 128
LN_EPS = 1e-5
RMS_EPS = 1e-6

LANES = 128
SUBLANES = 8
BF16_SUBLANES = 16
VMEM_LIMIT_BYTES = 56 * 1024 * 1024

NT_DIMS = (((1,), (1,)), ((), ()))


def _params(*semantics):
    return pltpu.CompilerParams(dimension_semantics=semantics, vmem_limit_bytes=VMEM_LIMIT_BYTES)


def _row_tile(m, target):
    if m <= target:
        return m
    t = target
    while m % t:
        t //= 2
    return t


def _dot(a, b):
    return jnp.dot(a, b, preferred_element_type=F32)


def _dot_nt(a, b, precision=None):
    return lax.dot_general(a, b, NT_DIMS, precision=precision, preferred_element_type=F32)


def _sigmoid(x):
    return 1.0 / (1.0 + jnp.exp(-x))


def _layer_norm(z, g, b):
    mu = jnp.mean(z, axis=-1, keepdims=True)
    zc = z - mu
    var = jnp.mean(zc * zc, axis=-1, keepdims=True)
    return zc * lax.rsqrt(var + LN_EPS) * g + b


def _top_k_indices(gate, idf, n_ids, axis):
    picks = []
    for _ in range(MOBA_TOP_K):
        m = jnp.max(gate, axis=axis, keepdims=True)
        idx = jnp.min(jnp.where(gate == m, idf, float(n_ids)), axis=axis, keepdims=True)
        picks.append(jnp.where(m > NEG_INF, idx, -1.0))
        gate = jnp.where(idf == idx, NEG_INF, gate)
    return picks


CAST_BLOCK_BYTES = 4 * 1024 * 1024


def _cast_kernel(w_ref, o_ref):
    o_ref[...] = w_ref[...].astype(o_ref.dtype)


def _layer_bf16(w_stack, layer):
    _, rows, cols = w_stack.shape
    tr = _row_tile(rows, max(BF16_SUBLANES, 2 ** int(math.log2(CAST_BLOCK_BYTES // (cols * 4)))))
    return pl.pallas_call(
        _cast_kernel,
        out_shape=jax.ShapeDtypeStruct((rows, cols), BF16),
        grid=(rows // tr,),
        in_specs=[pl.BlockSpec((None, tr, cols), lambda i: (layer, i, 0))],
        out_specs=pl.BlockSpec((tr, cols), lambda i: (i, 0)),
        compiler_params=_params("parallel"),
        name="weight_to_bf16",
    )(w_stack)


def _qkv_rope_kernel(x_ref, wq_ref, wk_ref, wv_ref, cos_ref, sin_ref, q_ref, k_ref, v_ref, *, head_dim):
    x = x_ref[...]
    cos = cos_ref[...]
    sin = sin_ref[...]
    q = _dot(x, wq_ref[...])
    k = _dot(x, wk_ref[...])
    v_ref[...] = _dot(x, wv_ref[...])
    for h in range(q.shape[1] // head_dim):
        sl = slice(h * head_dim, (h + 1) * head_dim)
        qh = q[:, sl]
        kh = k[:, sl]
        q_ref[:, sl] = qh * cos + pltpu.roll(qh, head_dim // 2, axis=1) * sin
        k_ref[:, sl] = kh * cos + pltpu.roll(kh, head_dim // 2, axis=1) * sin


def _rope_tables(pos, head_dim):
    half = head_dim // 2
    inv = ROPE_THETA ** (-jnp.arange(half, dtype=F32) * (2.0 / head_dim))
    ang = pos.astype(F32)[:, None] * inv[None, :]
    cos, sin = jnp.cos(ang), jnp.sin(ang)
    return jnp.concatenate([cos, cos], -1), jnp.concatenate([-sin, sin], -1)


def _qkv_rope(x_bf, w_qkv_bf, cos, sin):
    m, d = x_bf.shape
    head_dim = d // N_HEADS
    tm = _row_tile(m, 1024)
    tn = 512
    nj = d // tn
    out = jax.ShapeDtypeStruct((m, d), F32)
    o_spec = pl.BlockSpec((tm, tn), lambda j, i: (i, j))
    return pl.pallas_call(
        functools.partial(_qkv_rope_kernel, head_dim=head_dim),
        out_shape=(out, out, out),
        grid=(nj, m // tm),
        in_specs=[
            pl.BlockSpec((tm, d), lambda j, i: (i, 0)),
            pl.BlockSpec((d, tn), lambda j, i: (0, j)),
            pl.BlockSpec((d, tn), lambda j, i: (0, nj + j)),
            pl.BlockSpec((d, tn), lambda j, i: (0, 2 * nj + j)),
            pl.BlockSpec((tm, head_dim), lambda j, i: (i, 0)),
            pl.BlockSpec((tm, head_dim), lambda j, i: (i, 0)),
        ],
        out_specs=(o_spec, o_spec, o_spec),
        compiler_params=_params("parallel", "parallel"),
        name="qkv_rope",
    )(x_bf, w_qkv_bf, w_qkv_bf, w_qkv_bf, cos, sin)


def _moba_prompt_kernel(q_ref, k_ref, v_ref, o_ref, kb_ref, vt_ref, km_ref, p_ref,
                        *, n_blocks, heads, kv_unroll, scale):
    qi = pl.program_id(2)
    blk = MOBA_BLOCK
    head_dim = q_ref.shape[1] // heads
    exp2_scale = scale * math.log2(math.e)

    @pl.when(qi == 0)
    def _():
        for h in range(heads):
            cols = slice(h * head_dim, (h + 1) * head_dim)
            for n in range(n_blocks):
                rows = slice(n * blk, (n + 1) * blk)
                kf = k_ref[rows, cols]
                km_ref[h, n:n + 1, :] = jnp.sum(kf, axis=0, keepdims=True) * (1.0 / blk)
                kb_ref[h, n] = kf.astype(BF16)
                vt_ref[h, n] = v_ref[rows, cols].T.astype(BF16)

    key_id = lax.broadcasted_iota(jnp.int32, (blk, blk), 0)
    qry_id = lax.broadcasted_iota(jnp.int32, (blk, blk), 1)
    blk_id = lax.broadcasted_iota(jnp.int32, (n_blocks, blk), 0)
    qtbs, picks, init = [], [], []
    for h in range(heads):
        qt = q_ref[:, h * head_dim:(h + 1) * head_dim].T
        gate = jnp.dot(km_ref[h], qt, precision=lax.Precision.HIGHEST, preferred_element_type=F32)
        gate = jnp.where(blk_id < qi, gate, NEG_INF)
        picks.append(_top_k_indices(gate, blk_id.astype(F32), n_blocks, axis=0))
        qtb = qt.astype(BF16)
        qtbs.append(qtb)
        st = jnp.where(key_id <= qry_id, _dot(kb_ref[h, qi], qtb), NEG_INF)
        m0 = jnp.max(st, axis=0, keepdims=True)
        p = jnp.exp2((st - m0) * exp2_scale)
        p_ref[h, 0] = p.astype(BF16)
        for u in range(1, kv_unroll):
            p_ref[h, u] = jnp.zeros((blk, blk), BF16)
        init.append((m0, jnp.sum(p, axis=0, keepdims=True), jnp.zeros((head_dim, blk), F32)))

    def pending_values(h, pending):
        return functools.reduce(lambda a, b: a + b, [_dot(vt_ref[h, nc], p_ref[h, u]) for u, nc in enumerate(pending)])

    def body(it, carry):
        state, pending = carry
        ids = [it * kv_unroll + u for u in range(kv_unroll)]
        clamped = [jnp.minimum(n, n_blocks - 1) for n in ids]
        scores = [[_dot(kb_ref[h, nc], qtbs[h]) for nc in clamped] for h in range(heads)]
        flushed = [pending_values(h, pending) for h in range(heads)]
        out = []
        for h in range(heads):
            m, l, acc = state[h]
            s1, s2, s3 = picks[h]
            m_new = m
            blocks = []
            for n, sn in zip(ids, scores[h]):
                nf = jnp.asarray(n, F32)
                picked = (s1 == nf) | (s2 == nf) | (s3 == nf)
                m_new = jnp.maximum(m_new, jnp.where(picked, jnp.max(sn, axis=0, keepdims=True), NEG_INF))
                blocks.append((picked, sn))
            alpha = jnp.exp2((m - m_new) * exp2_scale)
            l = alpha * l
            for u, (picked, sn) in enumerate(blocks):
                shift = jnp.where(picked, m_new, jnp.inf)
                pn = jnp.exp2((sn - shift) * exp2_scale)
                l = l + jnp.sum(pn, axis=0, keepdims=True)
                p_ref[h, u] = pn.astype(BF16)
            out.append((m_new, l, alpha * (acc + flushed[h])))
        return tuple(out), tuple(clamped)

    own = tuple(qi for _ in range(kv_unroll))
    final, pending = lax.fori_loop(0, (qi + kv_unroll - 1) // kv_unroll, body, (tuple(init), own))
    for h in range(heads):
        _, l, acc = final[h]
        o = (acc + pending_values(h, pending)) / l
        o_ref[:, h * head_dim:(h + 1) * head_dim] = o.T.astype(o_ref.dtype)


def _moba_prompt_attention(q, k, v, batch, heads=4, kv_unroll=2):
    m, d = q.shape
    seq = m // batch
    head_dim = d // N_HEADS
    assert seq % MOBA_BLOCK == 0 and seq // MOBA_BLOCK >= MOBA_TOP_K and N_HEADS % heads == 0
    n_blocks = seq // MOBA_BLOCK
    width = heads * head_dim
    kv_spec = pl.BlockSpec((seq, width), lambda b, h, i: (b, h))
    qo_spec = pl.BlockSpec((MOBA_BLOCK, width), lambda b, h, i: (b * n_blocks + i, h))
    return pl.pallas_call(
        functools.partial(_moba_prompt_kernel, n_blocks=n_blocks, heads=heads, kv_unroll=kv_unroll,
                          scale=head_dim ** -0.5),
        out_shape=jax.ShapeDtypeStruct((m, d), BF16),
        grid=(batch, N_HEADS // heads, n_blocks),
        in_specs=[qo_spec, kv_spec, kv_spec],
        out_specs=qo_spec,
        scratch_shapes=[
            pltpu.VMEM((heads, n_blocks, MOBA_BLOCK, head_dim), BF16),
            pltpu.VMEM((heads, n_blocks, head_dim, MOBA_BLOCK), BF16),
            pltpu.VMEM((heads, n_blocks, head_dim), F32),
            pltpu.VMEM((heads, kv_unroll, MOBA_BLOCK, MOBA_BLOCK), BF16),
        ],
        compiler_params=_params("parallel", "parallel", "arbitrary"),
        name="moba_prompt_attention",
    )(q, k, v)


def _sample_stream_kernel(pt_ref, q_ref, bias_ref, *refs, blocks_per_step, pages_per_block, scale):
    del pt_ref
    n_pages = blocks_per_step * pages_per_block
    k_refs, v_refs = refs[:n_pages], refs[n_pages:2 * n_pages]
    km_ref, m_ref, l_ref, acc_ref = refs[2 * n_pages:]
    page, n_heads, head_dim = k_refs[0].shape
    exp2_scale = scale * math.log2(math.e)
    qb = q_ref[...].astype(BF16)
    bias = bias_ref[...]
    step = pl.program_id(1)
    lane = lax.broadcasted_iota(jnp.int32, m_ref.shape, 1)

    @pl.when(step == 0)
    def _():
        m_ref[...] = jnp.zeros_like(m_ref)
        l_ref[...] = jnp.zeros_like(l_ref)

    for j in range(blocks_per_step):
        pages = range(j * pages_per_block, (j + 1) * pages_per_block)
        k_sum = None
        scores = []
        for pg in pages:
            kf = k_refs[pg][...]
            k_sum = jnp.sum(kf, axis=0) if k_sum is None else k_sum + jnp.sum(kf, axis=0)
            scores.append(_dot_nt(qb, kf.reshape(page * n_heads, head_dim).astype(BF16)) + bias)
        km_ref[j] = k_sum * (1.0 / (page * pages_per_block))
        m = functools.reduce(jnp.maximum, [jnp.max(s, axis=1, keepdims=True) for s in scores])
        l = None
        acc = None
        for pg, s in zip(pages, scores):
            p = jnp.exp2((s - m) * exp2_scale)
            pv = _dot(p.astype(BF16), v_refs[pg][...].reshape(page * n_heads, head_dim).astype(BF16))
            ps = jnp.sum(p, axis=1, keepdims=True)
            l, acc = (ps, pv) if l is None else (l + ps, acc + pv)
        acc_ref[j] = acc
        mine = lane == step * blocks_per_step + j
        m_ref[...] = jnp.where(mine, m, m_ref[...])
        l_ref[...] = jnp.where(mine, l, l_ref[...])


def _sample_stream(q_rows, cache_k, cache_v, layer, page_table, blocks_per_step=4):
    n_seq, rows, head_dim = q_rows.shape
    page, n_heads = cache_k.shape[2], cache_k.shape[3]
    assert MOBA_BLOCK % page == 0
    ppb = MOBA_BLOCK // page
    n_blk = page_table.shape[1] // ppb
    assert n_blk % blocks_per_step == 0 and n_blk <= LANES
    n_pages = blocks_per_step * ppb
    head_of_row = np.arange(rows)[:, None] % n_heads
    head_of_key = np.arange(page * n_heads)[None, :] % n_heads
    bias = jnp.asarray(np.where(head_of_row == head_of_key, 0.0, -np.inf), dtype=F32)

    def page_spec(j):
        return pl.BlockSpec((None, None, page, n_heads, head_dim),
                            lambda b, n, pt: (layer, pt[b, n * n_pages + j], 0, 0, 0))

    part = jax.ShapeDtypeStruct((n_seq, n_blk, rows, head_dim), F32)
    part_spec = pl.BlockSpec((None, blocks_per_step, rows, head_dim), lambda b, n, pt: (b, n, 0, 0))
    stat = jax.ShapeDtypeStruct((n_seq, rows, LANES), F32)
    stat_spec = pl.BlockSpec((None, rows, LANES), lambda b, n, pt: (b, 0, 0))
    page_specs = [page_spec(j) for j in range(n_pages)]
    return pl.pallas_call(
        functools.partial(_sample_stream_kernel, blocks_per_step=blocks_per_step, pages_per_block=ppb,
                          scale=head_dim ** -0.5),
        out_shape=(jax.ShapeDtypeStruct((n_seq, n_blk, n_heads, head_dim), F32), stat, stat, part),
        grid_spec=pltpu.PrefetchScalarGridSpec(
            num_scalar_prefetch=1,
            grid=(n_seq, n_blk // blocks_per_step),
            in_specs=[
                pl.BlockSpec((None, rows, head_dim), lambda b, n, pt: (b, 0, 0)),
                pl.BlockSpec(bias.shape, lambda b, n, pt: (0, 0)),
            ] + page_specs + page_specs,
            out_specs=(
                pl.BlockSpec((None, blocks_per_step, n_heads, head_dim), lambda b, n, pt: (b, n, 0, 0)),
                stat_spec, stat_spec, part_spec,
            ),
        ),
        compiler_params=_params("parallel", "arbitrary"),
        name="sample_cache_stream",
    )(page_table, q_rows, bias, *([cache_k] * n_pages), *([cache_v] * n_pages))


def _sample_combine_kernel(q_ref, km_ref, kn_ref, vn_ref, m_ref, l_ref, acc_ref, o_ref, kmx_ref,
                           *, n_heads, q_blk, scale):
    n_blk = acc_ref.shape[0]
    n_past = n_blk * n_heads
    qf = q_ref[...]
    kn = kn_ref[...]
    rows, head_dim = qf.shape

    kmx_ref[0:n_past, :] = km_ref[...]
    own_sum = jnp.sum(kn.reshape(rows // n_heads, n_heads, head_dim), axis=0)
    kmx_ref[n_past:n_past + n_heads, :] = own_sum * (1.0 / MOBA_BLOCK)
    gate = _dot_nt(qf, kmx_ref[...], precision=lax.Precision.HIGHEST)
    col = lax.broadcasted_iota(jnp.int32, gate.shape, 1)
    row = lax.broadcasted_iota(jnp.int32, gate.shape, 0)
    valid = ((col % n_heads) == (row % n_heads)) & ((col // n_heads) < q_blk)
    gate = jnp.where(valid, gate, NEG_INF)
    picks = _top_k_indices(gate, col.astype(F32), gate.shape[1], axis=1)
    blk_lane = lax.broadcasted_iota(jnp.int32, m_ref.shape, 1).astype(F32)
    picked = functools.reduce(jnp.logical_or, [blk_lane == jnp.floor(s * (1.0 / n_heads)) for s in picks])

    so = _dot_nt(qf.astype(BF16), kn.astype(BF16)) * scale
    r2 = lax.broadcasted_iota(jnp.int32, so.shape, 0)
    c2 = lax.broadcasted_iota(jnp.int32, so.shape, 1)
    ok = ((c2 % n_heads) == (r2 % n_heads)) & ((c2 // n_heads) <= (r2 // n_heads))
    so = jnp.where(ok, so, NEG_INF)
    m_own = jnp.max(so, axis=1, keepdims=True)

    m_blk = m_ref[...] * scale
    m_all = jnp.maximum(m_own, jnp.max(jnp.where(picked, m_blk, NEG_INF), axis=1, keepdims=True))
    w = jnp.where(picked, jnp.exp(m_blk - m_all), 0.0)
    p_own = jnp.exp(so - m_all)
    l = jnp.sum(p_own, axis=1, keepdims=True) + jnp.sum(w * l_ref[...], axis=1, keepdims=True)
    acc = _dot(p_own.astype(BF16), vn_ref[...].astype(BF16))
    for n in range(n_blk):
        acc = acc + w[:, n:n + 1] * acc_ref[n]
    o_ref[...] = (acc / l).astype(o_ref.dtype)


def _sample_combine(q_rows, km, k_rows, v_rows, m_blk, l_blk, acc, q_blk):
    n_seq, rows, head_dim = q_rows.shape
    n_blk, n_heads = km.shape[1], km.shape[2]
    seq_spec = pl.BlockSpec((None, rows, head_dim), lambda b: (b, 0, 0))
    stat_spec = pl.BlockSpec((None, rows, m_blk.shape[2]), lambda b: (b, 0, 0))
    part_spec = pl.BlockSpec((None, n_blk, rows, head_dim), lambda b: (b, 0, 0, 0))
    return pl.pallas_call(
        functools.partial(_sample_combine_kernel, n_heads=n_heads, q_blk=q_blk, scale=head_dim ** -0.5),
        out_shape=jax.ShapeDtypeStruct((n_seq, rows, head_dim), BF16),
        grid=(n_seq,),
        in_specs=[
            seq_spec,
            pl.BlockSpec((None, n_blk * n_heads, head_dim), lambda b: (b, 0, 0)),
            seq_spec, seq_spec, stat_spec, stat_spec, part_spec,
        ],
        out_specs=seq_spec,
        scratch_shapes=[pltpu.VMEM(((n_blk + 1) * n_heads, head_dim), F32)],
        compiler_params=_params("parallel"),
        name="sample_select_combine",
    )(q_rows, km.reshape(n_seq, n_blk * n_heads, head_dim), k_rows, v_rows, m_blk, l_blk, acc)


def _proj_ln_kernel(x_ref, w_ref, r_ref, g_ref, b_ref, y_ref, yb_ref, *, alpha):
    tm = x_ref.shape[0]
    n_parts = 2 if tm % (2 * BF16_SUBLANES) == 0 else 1
    for part in range(n_parts):
        rows = slice(part * tm // n_parts, (part + 1) * tm // n_parts)
        z = alpha * r_ref[rows, :] + _dot(x_ref[rows, :], w_ref[...])
        y = _layer_norm(z, g_ref[...], b_ref[...])
        y_ref[rows, :] = y
        yb_ref[rows, :] = y.astype(BF16)


def _proj_ln(x_bf, w_bf, resid, g, b, alpha):
    m, d_in = x_bf.shape
    d = w_bf.shape[1]
    tm = _row_tile(m, 512)
    row = lambda width: pl.BlockSpec((tm, width), lambda i: (i, 0))
    vec = pl.BlockSpec((1, d), lambda i: (0, 0))
    return pl.pallas_call(
        functools.partial(_proj_ln_kernel, alpha=alpha),
        out_shape=(jax.ShapeDtypeStruct((m, d), F32), jax.ShapeDtypeStruct((m, d), BF16)),
        grid=(m // tm,),
        in_specs=[row(d_in), pl.BlockSpec((d_in, d), lambda i: (0, 0)), row(d), vec, vec],
        out_specs=(row(d), row(d)),
        compiler_params=_params("parallel"),
        name="proj_residual_ln",
    )(x_bf, w_bf, resid, g.reshape(1, d), b.reshape(1, d))


def _ffn_ln_kernel(x_ref, wg_ref, wu_ref, wo_ref, r_ref, g_ref, b_ref, y_ref, yb_ref, acc_ref, *, alpha):
    f = pl.program_id(1)

    @pl.when(f == 0)
    def _():
        acc_ref[...] = jnp.zeros_like(acc_ref)

    x = x_ref[...]
    gate = _dot(x, wg_ref[...])
    up = _dot(x, wu_ref[...])
    hidden = (gate * _sigmoid(gate) * up).astype(BF16)
    acc_ref[...] += _dot(hidden, wo_ref[...])

    @pl.when(f == pl.num_programs(1) - 1)
    def _():
        y = _layer_norm(alpha * r_ref[...] + acc_ref[...], g_ref[...], b_ref[...])
        y_ref[...] = y
        yb_ref[...] = y.astype(BF16)


def _ffn_ln(x_bf, w_in_bf, w_out_bf, resid, g, b, alpha):
    m, d = x_bf.shape
    d_ff = w_out_bf.shape[0]
    tm = _row_tile(m, 512)
    tf = 512
    assert d_ff % tf == 0
    nf = d_ff // tf
    row = lambda: pl.BlockSpec((tm, d), lambda i, f: (i, 0))
    vec = pl.BlockSpec((1, d), lambda i, f: (0, 0))
    return pl.pallas_call(
        functools.partial(_ffn_ln_kernel, alpha=alpha),
        out_shape=(jax.ShapeDtypeStruct((m, d), F32), jax.ShapeDtypeStruct((m, d), BF16)),
        grid=(m // tm, nf),
        in_specs=[
            row(),
            pl.BlockSpec((d, tf), lambda i, f: (0, f)),
            pl.BlockSpec((d, tf), lambda i, f: (0, nf + f)),
            pl.BlockSpec((tf, d), lambda i, f: (f, 0)),
            row(), vec, vec,
        ],
        out_specs=(row(), row()),
        scratch_shapes=[pltpu.VMEM((tm, d), F32)],
        compiler_params=_params("parallel", "arbitrary"),
        name="swiglu_residual_ln",
    )(x_bf, w_in_bf, w_in_bf, w_out_bf, resid, g.reshape(1, d), b.reshape(1, d))


def _matmul_kernel(x_ref, w_ref, o_ref):
    o_ref[...] = _dot(x_ref[...], w_ref[...])


def _matmul(x_bf, w_bf):
    m, d = x_bf.shape
    n_out = w_bf.shape[1]
    tm = _row_tile(m, 1024)
    tn = 1024
    return pl.pallas_call(
        _matmul_kernel,
        out_shape=jax.ShapeDtypeStruct((m, n_out), F32),
        grid=(n_out // tn, m // tm),
        in_specs=[pl.BlockSpec((tm, d), lambda j, i: (i, 0)), pl.BlockSpec((d, tn), lambda j, i: (0, j))],
        out_specs=pl.BlockSpec((tm, tn), lambda j, i: (i, j)),
        compiler_params=_params("parallel", "parallel"),
        name="hgrn_in_proj",
    )(x_bf, w_bf)


def _hgrn_level_matrix(chunk):
    n_lev = int(math.log2(chunk))
    assert 2 ** n_lev == chunk
    t = np.arange(chunk)[:, None]
    s = np.arange(chunk)[None, :]
    mats = [s <= t]
    for lev in range(1, n_lev + 1):
        size = 2 ** lev
        mats.append(s < (t // size) * size + size // 2)
    return np.concatenate(mats, 0).astype(np.float32), n_lev


def _hgrn_kernel(q_ref, f_ref, i_ref, g_ref, lbl_ref, ng_ref, cm_ref, s0_ref, o_ref, s_ref, st_ref,
                 *, layer, n_valid, n_lev, n_heads, head_group):
    c = pl.program_id(1)
    chunk, d = q_ref.shape
    dk = d // n_heads

    @pl.when(c == 0)
    def _():
        for h in range(n_heads):
            st_ref[h] = s0_ref[h].T

    logits = lbl_ref[...]
    lrow = lax.broadcasted_iota(jnp.int32, logits.shape, 0)
    e = jnp.exp(logits - jnp.max(logits, axis=0, keepdims=True))
    sm = e / jnp.sum(e, axis=0, keepdims=True)
    lb = jnp.sum(jnp.where((lrow >= 1) & (lrow <= layer), sm, 0.0), axis=0, keepdims=True)

    row = lax.broadcasted_iota(jnp.int32, (chunk, d), 0)
    qraw = q_ref[...]
    q = qraw * _sigmoid(qraw)
    f = lb + (1.0 - lb) * _sigmoid(f_ref[...])
    k = 1.0 - f
    g = jnp.log(f)
    if n_valid < chunk:
        live = row < n_valid
        k = jnp.where(live, k, 0.0)
        g = jnp.where(live, g, 0.0)
    v = i_ref[...]
    vb = v.astype(BF16)
    graw = g_ref[...]
    out_gate = graw * _sigmoid(graw)

    cm = cm_ref[...]
    g1 = g.astype(BF16)
    r1 = g - g1.astype(F32)
    g2 = r1.astype(BF16)
    g3 = (r1 - g2.astype(F32)).astype(BF16)
    b_all = _dot(cm, jnp.concatenate([g1, g2, g3], axis=0))
    b = b_all[0:chunk]

    t_id = lax.broadcasted_iota(jnp.int32, (chunk, chunk), 0)
    s_id = lax.broadcasted_iota(jnp.int32, (chunk, chunk), 1)
    row_h = lax.broadcasted_iota(jnp.int32, (chunk, dk), 0)
    qb = q.astype(BF16)
    kb = k.astype(BF16)
    decays, rights, pairs = [], [], []
    for lev in range(1, n_lev + 1):
        size = 2 ** lev
        half = size // 2
        decays.append(jnp.exp(-jnp.abs(b - b_all[lev * chunk:(lev + 1) * chunk])))
        rights.append((row_h % size) >= half)
        pairs.append(((t_id // size) == (s_id // size)) & ((t_id % size) >= half) & ((s_id % size) < half))

    q_decayed = (q * jnp.exp(b)).astype(BF16)
    b_last = b[chunk - 1:chunk, :]
    k_decayed = (k * jnp.exp(b_last - b)).astype(BF16)
    state_decay = jnp.exp(b_last)
    ng = ng_ref[...]

    def issue_matmuls(h):
        hs = slice(h * dk, (h + 1) * dk)
        st = st_ref[h]
        qh, kh = q[:, hs], k[:, hs]
        factors = [(jnp.where(right, qh, kh) * decay[:, hs]).astype(BF16) for right, decay in zip(rights, decays)]
        return (hs, st, _dot_nt(qb[:, hs], kb[:, hs]), [_dot_nt(x, x) for x in factors],
                _dot_nt(q_decayed[:, hs], st.astype(BF16)), _dot(v[:, hs].T.astype(BF16), k_decayed[:, hs]))

    def finish(h, issued):
        hs, st, diag, level, inter, update = issued
        a = jnp.where(t_id == s_id, diag, 0.0)
        for prod, pair in zip(level, pairs):
            a = a + jnp.where(pair, prod, 0.0)
        o = inter + _dot(a.astype(BF16), vb[:, hs])
        st_ref[h] = st * state_decay[:, hs] + update
        o = o * lax.rsqrt(jnp.mean(o * o, axis=-1, keepdims=True) + RMS_EPS) * ng
        o_ref[:, hs] = (o * out_gate[:, hs]).astype(o_ref.dtype)

    groups = [range(g, min(g + head_group, n_heads)) for g in range(0, n_heads, head_group)]
    issued = {h: issue_matmuls(h) for h in groups[0]}
    for gi, grp in enumerate(groups):
        if gi + 1 < len(groups):
            issued.update({h: issue_matmuls(h) for h in groups[gi + 1]})
        for h in grp:
            finish(h, issued.pop(h))

    @pl.when(c == pl.num_programs(1) - 1)
    def _():
        for h in range(n_heads):
            s_ref[h] = st_ref[h].T


def _hgrn_recurrence(y, s0, lb_logits, norm_g, layer, n_valid, chunk, head_group=4):
    batch, lp, d4 = y.shape
    d = d4 // 4
    _, n_heads, dk, dv = s0.shape
    assert n_heads * dk == d and dk == dv
    depth = lb_logits.shape[0]
    cm_np, n_lev = _hgrn_level_matrix(chunk)
    cm = jnp.asarray(np.concatenate([cm_np] * 3, axis=1), dtype=BF16)

    def quarter(qt):
        return pl.BlockSpec((None, chunk, d), lambda b, c: (b, c, qt))

    state_spec = pl.BlockSpec((None, n_heads, dk, dv), lambda b, c: (b, 0, 0, 0))
    return pl.pallas_call(
        functools.partial(_hgrn_kernel, layer=layer, n_valid=n_valid, n_lev=n_lev, n_heads=n_heads,
                          head_group=head_group),
        out_shape=(jax.ShapeDtypeStruct((batch, lp, d), BF16),
                   jax.ShapeDtypeStruct((batch, n_heads, dk, dv), F32)),
        grid=(batch, lp // chunk),
        in_specs=[
            quarter(0), quarter(1), quarter(2), quarter(3),
            pl.BlockSpec((depth, d), lambda b, c: (0, 0)),
            pl.BlockSpec((1, dv), lambda b, c: (0, 0)),
            pl.BlockSpec(cm.shape, lambda b, c: (0, 0)),
            state_spec,
        ],
        out_specs=(pl.BlockSpec((None, chunk, d), lambda b, c: (b, c, 0)), state_spec),
        scratch_shapes=[pltpu.VMEM((n_heads, dv, dk), F32)],
        compiler_params=_params("parallel", "arbitrary"),
        name="hgrn_recurrence",
    )(y, y, y, y, lb_logits, norm_g.reshape(1, dv), cm, s0)


def _moba_layer(xp, xp_bf, xs, xs_bf, batch, n_seq, cache_k, cache_v, layer, page_table, w_qkv_bf, w_o_bf,
                ln_g, ln_b, alpha):
    d = xp.shape[-1]
    head_dim = d // N_HEADS
    seq = xp.shape[0] // batch
    n_new = xs.shape[0] // n_seq
    past = page_table.shape[1] * cache_k.shape[2]
    assert past % MOBA_BLOCK == 0 and n_new <= MOBA_BLOCK and cache_k.shape[3] == N_HEADS

    cos_p, sin_p = _rope_tables(jnp.arange(seq, dtype=jnp.int32), head_dim)
    qp, kp, vp = _qkv_rope(xp_bf, w_qkv_bf, jnp.tile(cos_p, (batch, 1)), jnp.tile(sin_p, (batch, 1)))
    op = _moba_prompt_attention(qp, kp, vp, batch)
    yp, yp_bf = _proj_ln(op, w_o_bf, xp, ln_g, ln_b, alpha)

    cos_s, sin_s = _rope_tables(past + jnp.arange(n_new, dtype=jnp.int32), head_dim)
    qs, ks, vs = _qkv_rope(xs_bf, w_qkv_bf, jnp.tile(cos_s, (n_seq, 1)), jnp.tile(sin_s, (n_seq, 1)))
    rows = lambda a: a.reshape(n_seq, n_new * N_HEADS, head_dim)
    km, m_blk, l_blk, acc = _sample_stream(rows(qs), cache_k, cache_v, layer, page_table)
    os_ = _sample_combine(rows(qs), km, rows(ks), rows(vs), m_blk, l_blk, acc, past // MOBA_BLOCK)
    ys, ys_bf = _proj_ln(os_.reshape(n_seq * n_new, d), w_o_bf, xs, ln_g, ln_b, alpha)

    kv_shape = lambda n, l: (n, l, N_HEADS, head_dim)
    return (yp, yp_bf, ys, ys_bf, kp.reshape(kv_shape(batch, seq)), vp.reshape(kv_shape(batch, seq)),
            ks.reshape(kv_shape(n_seq, n_new)), vs.reshape(kv_shape(n_seq, n_new)))


def _hgrn_group(x, x_bf, n_seq, s0, w_in_bf, w_o_bf, lb_logits, norm_g, layer, ln_g, ln_b, alpha):
    m, d = x.shape
    n_heads = d // HGRN_EXPAND
    seq = m // n_seq
    y = _matmul(x_bf, w_in_bf).reshape(n_seq, seq, 4 * d)
    if seq % HGRN_CHUNK == 0:
        chunk, lp = HGRN_CHUNK, seq
    else:
        chunk = lp = max(BF16_SUBLANES, int(2 ** math.ceil(math.log2(seq))))
        y = jnp.pad(y, ((0, 0), (0, lp - seq), (0, 0)))
    o, s = _hgrn_recurrence(y, s0, lb_logits, norm_g, layer, min(seq, chunk), chunk)
    o = o[:, :seq].reshape(m, d)
    y_out, y_out_bf = _proj_ln(o, w_o_bf, x, ln_g, ln_b, alpha)
    return y_out, y_out_bf, s


def kernel(x_prompt, x_sample, cache_k, cache_v, state_hgrn, page_table, attn_w_qkv, attn_w_o, hgrn_w_in,
           hgrn_lb_logits, hgrn_norm_g, hgrn_w_o, ffn_w_in, ffn_w_out, ln_mix_g, ln_mix_b, ln_ffn_g, ln_ffn_b):
    batch, seq, d = x_prompt.shape
    n_seq, n_new, _ = x_sample.shape
    depth = ffn_w_in.shape[0]
    alpha = (2.0 * depth) ** 0.25
    n_hgrn_heads = d // HGRN_EXPAND

    yp = x_prompt.reshape(batch * seq, d)
    ys = x_sample.reshape(n_seq * n_new, d)
    yp_bf, ys_bf = yp.astype(BF16), ys.astype(BF16)
    pk, pv, ps, sk, sv, ss = [], [], [], [], [], []
    for i in range(depth):
        if i % 2 == 0:
            a = i // 2
            yp, yp_bf, ys, ys_bf, kp, vp, kn, vn = _moba_layer(
                yp, yp_bf, ys, ys_bf, batch, n_seq, cache_k, cache_v, a, page_table,
                _layer_bf16(attn_w_qkv, a), _layer_bf16(attn_w_o, a), ln_mix_g[i], ln_mix_b[i], alpha)
            pk.append(kp)
            pv.append(vp)
            sk.append(kn)
            sv.append(vn)
        else:
            r = i // 2
            w_in_bf, w_o_bf = _layer_bf16(hgrn_w_in, r), _layer_bf16(hgrn_w_o, r)
            s0p = jnp.zeros((batch, n_hgrn_heads, HGRN_EXPAND, d // n_hgrn_heads), state_hgrn.dtype)
            yp, yp_bf, sp = _hgrn_group(yp, yp_bf, batch, s0p, w_in_bf, w_o_bf, hgrn_lb_logits, hgrn_norm_g[r],
                                        i, ln_mix_g[i], ln_mix_b[i], alpha)
            ys, ys_bf, sn = _hgrn_group(ys, ys_bf, n_seq, state_hgrn[r], w_in_bf, w_o_bf, hgrn_lb_logits,
                                        hgrn_norm_g[r], i, ln_mix_g[i], ln_mix_b[i], alpha)
            ps.append(sp)
            ss.append(sn)
        w_in_bf, w_out_bf = _layer_bf16(ffn_w_in, i), _layer_bf16(ffn_w_out, i)
        yp, yp_bf = _ffn_ln(yp_bf, w_in_bf, w_out_bf, yp, ln_ffn_g[i], ln_ffn_b[i], alpha)
        ys, ys_bf = _ffn_ln(ys_bf, w_in_bf, w_out_bf, ys, ln_ffn_g[i], ln_ffn_b[i], alpha)
    return (yp.reshape(batch, seq, d), ys.reshape(n_seq, n_new, d), jnp.stack(pk), jnp.stack(pv), jnp.stack(ps),
            jnp.stack(sk), jnp.stack(sv), jnp.stack(ss))
```

```python
import functools
import math

import numpy as np
import jax
import jax.numpy as jnp
from jax import lax
from jax.experimental import pallas as pl
from jax.experimental.pallas import tpu as pltpu

F32 = jnp.float32
BF16 = jnp.bfloat16
NEG_INF = float("-inf")

N_HEADS = 16
MOBA_BLOCK = 256
MOBA_TOP_K = 3
ROPE_THETA = 10000.0
HGRN_EXPAND = 128
HGRN_CHUNK = 128
LN_EPS = 1e-5
RMS_EPS = 1e-6

LANES = 128
SUBLANES = 8
BF16_SUBLANES = 16
VMEM_LIMIT_BYTES = 56 * 1024 * 1024

NT_DIMS = (((1,), (1,)), ((), ()))


def _params(*semantics):
    return pltpu.CompilerParams(dimension_semantics=semantics, vmem_limit_bytes=VMEM_LIMIT_BYTES)


def _row_tile(m, target):
    if m <= target:
        return m
    t = target
    while m % t:
        t //= 2
    return t


def _dot(a, b):
    return jnp.dot(a, b, preferred_element_type=F32)


def _dot_nt(a, b, precision=None):
    return lax.dot_general(a, b, NT_DIMS, precision=precision, preferred_element_type=F32)


def _sigmoid(x):
    return 1.0 / (1.0 + jnp.exp(-x))


def _layer_norm(z, g, b):
    mu = jnp.mean(z, axis=-1, keepdims=True)
    zc = z - mu
    var = jnp.mean(zc * zc, axis=-1, keepdims=True)
    return zc * lax.rsqrt(var + LN_EPS) * g + b


def _top_k_indices(gate, idf, n_ids, axis):
    picks = []
    for _ in range(MOBA_TOP_K):
        m = jnp.max(gate, axis=axis, keepdims=True)
        idx = jnp.min(jnp.where(gate == m, idf, float(n_ids)), axis=axis, keepdims=True)
        picks.append(jnp.where(m > NEG_INF, idx, -1.0))
        gate = jnp.where(idf == idx, NEG_INF, gate)
    return picks


CAST_BLOCK_BYTES = 4 * 1024 * 1024


def _cast_kernel(w_ref, o_ref):
    o_ref[...] = w_ref[...].astype(o_ref.dtype)


def _layer_bf16(w_stack, layer):
    _, rows, cols = w_stack.shape
    tr = _row_tile(rows, max(BF16_SUBLANES, 2 ** int(math.log2(CAST_BLOCK_BYTES // (cols * 4)))))
    return pl.pallas_call(
        _cast_kernel,
        out_shape=jax.ShapeDtypeStruct((rows, cols), BF16),
        grid=(rows // tr,),
        in_specs=[pl.BlockSpec((None, tr, cols), lambda i: (layer, i, 0))],
        out_specs=pl.BlockSpec((tr, cols), lambda i: (i, 0)),
        compiler_params=_params("parallel"),
        name="weight_to_bf16",
    )(w_stack)


def _qkv_rope_kernel(x_ref, wq_ref, wk_ref, wv_ref, cos_ref, sin_ref, q_ref, k_ref, v_ref, *, head_dim):
    x = x_ref[...]
    cos = cos_ref[...]
    sin = sin_ref[...]
    q = _dot(x, wq_ref[...])
    k = _dot(x, wk_ref[...])
    v_ref[...] = _dot(x, wv_ref[...])
    for h in range(q.shape[1] // head_dim):
        sl = slice(h * head_dim, (h + 1) * head_dim)
        qh = q[:, sl]
        kh = k[:, sl]
        q_ref[:, sl] = qh * cos + pltpu.roll(qh, head_dim // 2, axis=1) * sin
        k_ref[:, sl] = kh * cos + pltpu.roll(kh, head_dim // 2, axis=1) * sin


def _rope_tables(pos, head_dim):
    half = head_dim // 2
    inv = ROPE_THETA ** (-jnp.arange(half, dtype=F32) * (2.0 / head_dim))
    ang = pos.astype(F32)[:, None] * inv[None, :]
    cos, sin = jnp.cos(ang), jnp.sin(ang)
    return jnp.concatenate([cos, cos], -1), jnp.concatenate([-sin, sin], -1)


def _qkv_rope(x_bf, w_qkv_bf, cos, sin):
    m, d = x_bf.shape
    head_dim = d // N_HEADS
    tm = _row_tile(m, 1024)
    tn = 512
    nj = d // tn
    out = jax.ShapeDtypeStruct((m, d), F32)
    o_spec = pl.BlockSpec((tm, tn), lambda j, i: (i, j))
    return pl.pallas_call(
        functools.partial(_qkv_rope_kernel, head_dim=head_dim),
        out_shape=(out, out, out),
        grid=(nj, m // tm),
        in_specs=[
            pl.BlockSpec((tm, d), lambda j, i: (i, 0)),
            pl.BlockSpec((d, tn), lambda j, i: (0, j)),
            pl.BlockSpec((d, tn), lambda j, i: (0, nj + j)),
            pl.BlockSpec((d, tn), lambda j, i: (0, 2 * nj + j)),
            pl.BlockSpec((tm, head_dim), lambda j, i: (i, 0)),
            pl.BlockSpec((tm, head_dim), lambda j, i: (i, 0)),
        ],
        out_specs=(o_spec, o_spec, o_spec),
        compiler_params=_params("parallel", "parallel"),
        name="qkv_rope",
    )(x_bf, w_qkv_bf, w_qkv_bf, w_qkv_bf, cos, sin)


def _moba_prompt_kernel(q_ref, k_ref, v_ref, o_ref, kb_ref, vt_ref, km_ref, p_ref,
                        *, n_blocks, heads, kv_unroll, scale):
    qi = pl.program_id(2)
    blk = MOBA_BLOCK
    head_dim = q_ref.shape[1] // heads
    exp2_scale = scale * math.log2(math.e)

    @pl.when(qi == 0)
    def _():
        for h in range(heads):
            cols = slice(h * head_dim, (h + 1) * head_dim)
            for n in range(n_blocks):
                rows = slice(n * blk, (n + 1) * blk)
                kf = k_ref[rows, cols]
                km_ref[h, n:n + 1, :] = jnp.sum(kf, axis=0, keepdims=True) * (1.0 / blk)
                kb_ref[h, n] = kf.astype(BF16)
                vt_ref[h, n] = v_ref[rows, cols].T.astype(BF16)

    key_id = lax.broadcasted_iota(jnp.int32, (blk, blk), 0)
    qry_id = lax.broadcasted_iota(jnp.int32, (blk, blk), 1)
    blk_id = lax.broadcasted_iota(jnp.int32, (n_blocks, blk), 0)
    qtbs, picks, init = [], [], []
    for h in range(heads):
        qt = q_ref[:, h * head_dim:(h + 1) * head_dim].T
        gate = jnp.dot(km_ref[h], qt, precision=lax.Precision.HIGHEST, preferred_element_type=F32)
        gate = jnp.where(blk_id < qi, gate, NEG_INF)
        picks.append(_top_k_indices(gate, blk_id.astype(F32), n_blocks, axis=0))
        qtb = qt.astype(BF16)
        qtbs.append(qtb)
        st = jnp.where(key_id <= qry_id, _dot(kb_ref[h, qi], qtb), NEG_INF)
        m0 = jnp.max(st, axis=0, keepdims=True)
        p = jnp.exp2((st - m0) * exp2_scale)
        p_ref[h, 0] = p.astype(BF16)
        for u in range(1, kv_unroll):
            p_ref[h, u] = jnp.zeros((blk, blk), BF16)
        init.append((m0, jnp.sum(p, axis=0, keepdims=True), jnp.zeros((head_dim, blk), F32)))

    def pending_values(h, pending):
        return functools.reduce(lambda a, b: a + b, [_dot(vt_ref[h, nc], p_ref[h, u]) for u, nc in enumerate(pending)])

    def body(it, carry):
        state, pending = carry
        ids = [it * kv_unroll + u for u in range(kv_unroll)]
        clamped = [jnp.minimum(n, n_blocks - 1) for n in ids]
        scores = [[_dot(kb_ref[h, nc], qtbs[h]) for nc in clamped] for h in range(heads)]
        flushed = [pending_values(h, pending) for h in range(heads)]
        out = []
        for h in range(heads):
            m, l, acc = state[h]
            s1, s2, s3 = picks[h]
            m_new = m
            blocks = []
            for n, sn in zip(ids, scores[h]):
                nf = jnp.asarray(n, F32)
                picked = (s1 == nf) | (s2 == nf) | (s3 == nf)
                m_new = jnp.maximum(m_new, jnp.where(picked, jnp.max(sn, axis=0, keepdims=True), NEG_INF))
                blocks.append((picked, sn))
            alpha = jnp.exp2((m - m_new) * exp2_scale)
            l = alpha * l
            for u, (picked, sn) in enumerate(blocks):
                shift = jnp.where(picked, m_new, jnp.inf)
                pn = jnp.exp2((sn - shift) * exp2_scale)
                l = l + jnp.sum(pn, axis=0, keepdims=True)
                p_ref[h, u] = pn.astype(BF16)
            out.append((m_new, l, alpha * (acc + flushed[h])))
        return tuple(out), tuple(clamped)

    own = tuple(qi for _ in range(kv_unroll))
    final, pending = lax.fori_loop(0, (qi + kv_unroll - 1) // kv_unroll, body, (tuple(init), own))
    for h in range(heads):
        _, l, acc = final[h]
        o = (acc + pending_values(h, pending)) / l
        o_ref[:, h * head_dim:(h + 1) * head_dim] = o.T.astype(o_ref.dtype)


def _moba_prompt_attention(q, k, v, batch, heads=4, kv_unroll=2):
    m, d = q.shape
    seq = m // batch
    head_dim = d // N_HEADS
    assert seq % MOBA_BLOCK == 0 and seq // MOBA_BLOCK >= MOBA_TOP_K and N_HEADS % heads == 0
    n_blocks = seq // MOBA_BLOCK
    width = heads * head_dim
    kv_spec = pl.BlockSpec((seq, width), lambda b, h, i: (b, h))
    qo_spec = pl.BlockSpec((MOBA_BLOCK, width), lambda b, h, i: (b * n_blocks + i, h))
    return pl.pallas_call(
        functools.partial(_moba_prompt_kernel, n_blocks=n_blocks, heads=heads, kv_unroll=kv_unroll,
                          scale=head_dim ** -0.5),
        out_shape=jax.ShapeDtypeStruct((m, d), BF16),
        grid=(batch, N_HEADS // heads, n_blocks),
        in_specs=[qo_spec, kv_spec, kv_spec],
        out_specs=qo_spec,
        scratch_shapes=[
            pltpu.VMEM((heads, n_blocks, MOBA_BLOCK, head_dim), BF16),
            pltpu.VMEM((heads, n_blocks, head_dim, MOBA_BLOCK), BF16),
            pltpu.VMEM((heads, n_blocks, head_dim), F32),
            pltpu.VMEM((heads, kv_unroll, MOBA_BLOCK, MOBA_BLOCK), BF16),
        ],
        compiler_params=_params("parallel", "parallel", "arbitrary"),
        name="moba_prompt_attention",
    )(q, k, v)


def _sample_stream_kernel(pt_ref, q_ref, bias_ref, *refs, blocks_per_step, pages_per_block, scale):
    del pt_ref
    n_pages = blocks_per_step * pages_per_block
    k_refs, v_refs = refs[:n_pages], refs[n_pages:2 * n_pages]
    km_ref, m_ref, l_ref, acc_ref = refs[2 * n_pages:]
    page, n_heads, head_dim = k_refs[0].shape
    exp2_scale = scale * math.log2(math.e)
    qb = q_ref[...].astype(BF16)
    bias = bias_ref[...]
    step = pl.program_id(1)
    lane = lax.broadcasted_iota(jnp.int32, m_ref.shape, 1)

    @pl.when(step == 0)
    def _():
        m_ref[...] = jnp.zeros_like(m_ref)
        l_ref[...] = jnp.zeros_like(l_ref)

    for j in range(blocks_per_step):
        pages = range(j * pages_per_block, (j + 1) * pages_per_block)
        k_sum = None
        scores = []
        for pg in pages:
            kf = k_refs[pg][...]
            k_sum = jnp.sum(kf, axis=0) if k_sum is None else k_sum + jnp.sum(kf, axis=0)
            scores.append(_dot_nt(qb, kf.reshape(page * n_heads, head_dim).astype(BF16)) + bias)
        km_ref[j] = k_sum * (1.0 / (page * pages_per_block))
        m = functools.reduce(jnp.maximum, [jnp.max(s, axis=1, keepdims=True) for s in scores])
        l = None
        acc = None
        for pg, s in zip(pages, scores):
            p = jnp.exp2((s - m) * exp2_scale)
            pv = _dot(p.astype(BF16), v_refs[pg][...].reshape(page * n_heads, head_dim).astype(BF16))
            ps = jnp.sum(p, axis=1, keepdims=True)
            l, acc = (ps, pv) if l is None else (l + ps, acc + pv)
        acc_ref[j] = acc
        mine = lane == step * blocks_per_step + j
        m_ref[...] = jnp.where(mine, m, m_ref[...])
        l_ref[...] = jnp.where(mine, l, l_ref[...])


def _sample_stream(q_rows, cache_k, cache_v, layer, page_table, blocks_per_step=4):
    n_seq, rows, head_dim = q_rows.shape
    page, n_heads = cache_k.shape[2], cache_k.shape[3]
    assert MOBA_BLOCK % page == 0
    ppb = MOBA_BLOCK // page
    n_blk = page_table.shape[1] // ppb
    assert n_blk % blocks_per_step == 0 and n_blk <= LANES
    n_pages = blocks_per_step * ppb
    head_of_row = np.arange(rows)[:, None] % n_heads
    head_of_key = np.arange(page * n_heads)[None, :] % n_heads
    bias = jnp.asarray(np.where(head_of_row == head_of_key, 0.0, -np.inf), dtype=F32)

    def page_spec(j):
        return pl.BlockSpec((None, None, page, n_heads, head_dim),
                            lambda b, n, pt: (layer, pt[b, n * n_pages + j], 0, 0, 0))

    part = jax.ShapeDtypeStruct((n_seq, n_blk, rows, head_dim), F32)
    part_spec = pl.BlockSpec((None, blocks_per_step, rows, head_dim), lambda b, n, pt: (b, n, 0, 0))
    stat = jax.ShapeDtypeStruct((n_seq, rows, LANES), F32)
    stat_spec = pl.BlockSpec((None, rows, LANES), lambda b, n, pt: (b, 0, 0))
    page_specs = [page_spec(j) for j in range(n_pages)]
    return pl.pallas_call(
        functools.partial(_sample_stream_kernel, blocks_per_step=blocks_per_step, pages_per_block=ppb,
                          scale=head_dim ** -0.5),
        out_shape=(jax.ShapeDtypeStruct((n_seq, n_blk, n_heads, head_dim), F32), stat, stat, part),
        grid_spec=pltpu.PrefetchScalarGridSpec(
            num_scalar_prefetch=1,
            grid=(n_seq, n_blk // blocks_per_step),
            in_specs=[
                pl.BlockSpec((None, rows, head_dim), lambda b, n, pt: (b, 0, 0)),
                pl.BlockSpec(bias.shape, lambda b, n, pt: (0, 0)),
            ] + page_specs + page_specs,
            out_specs=(
                pl.BlockSpec((None, blocks_per_step, n_heads, head_dim), lambda b, n, pt: (b, n, 0, 0)),
                stat_spec, stat_spec, part_spec,
            ),
        ),
        compiler_params=_params("parallel", "arbitrary"),
        name="sample_cache_stream",
    )(page_table, q_rows, bias, *([cache_k] * n_pages), *([cache_v] * n_pages))


def _sample_combine_kernel(q_ref, km_ref, kn_ref, vn_ref, m_ref, l_ref, acc_ref, o_ref, kmx_ref,
                           *, n_heads, q_blk, scale):
    n_blk = acc_ref.shape[0]
    n_past = n_blk * n_heads
    qf = q_ref[...]
    kn = kn_ref[...]
    rows, head_dim = qf.shape

    kmx_ref[0:n_past, :] = km_ref[...]
    own_sum = jnp.sum(kn.reshape(rows // n_heads, n_heads, head_dim), axis=0)
    kmx_ref[n_past:n_past + n_heads, :] = own_sum * (1.0 / MOBA_BLOCK)
    gate = _dot_nt(qf, kmx_ref[...], precision=lax.Precision.HIGHEST)
    col = lax.broadcasted_iota(jnp.int32, gate.shape, 1)
    row = lax.broadcasted_iota(jnp.int32, gate.shape, 0)
    valid = ((col % n_heads) == (row % n_heads)) & ((col // n_heads) < q_blk)
    gate = jnp.where(valid, gate, NEG_INF)
    picks = _top_k_indices(gate, col.astype(F32), gate.shape[1], axis=1)
    blk_lane = lax.broadcasted_iota(jnp.int32, m_ref.shape, 1).astype(F32)
    picked = functools.reduce(jnp.logical_or, [blk_lane == jnp.floor(s * (1.0 / n_heads)) for s in picks])

    so = _dot_nt(qf.astype(BF16), kn.astype(BF16)) * scale
    r2 = lax.broadcasted_iota(jnp.int32, so.shape, 0)
    c2 = lax.broadcasted_iota(jnp.int32, so.shape, 1)
    ok = ((c2 % n_heads) == (r2 % n_heads)) & ((c2 // n_heads) <= (r2 // n_heads))
    so = jnp.where(ok, so, NEG_INF)
    m_own = jnp.max(so, axis=1, keepdims=True)

    m_blk = m_ref[...] * scale
    m_all = jnp.maximum(m_own, jnp.max(jnp.where(picked, m_blk, NEG_INF), axis=1, keepdims=True))
    w = jnp.where(picked, jnp.exp(m_blk - m_all), 0.0)
    p_own = jnp.exp(so - m_all)
    l = jnp.sum(p_own, axis=1, keepdims=True) + jnp.sum(w * l_ref[...], axis=1, keepdims=True)
    acc = _dot(p_own.astype(BF16), vn_ref[...].astype(BF16))
    for n in range(n_blk):
        acc = acc + w[:, n:n + 1] * acc_ref[n]
    o_ref[...] = (acc / l).astype(o_ref.dtype)


def _sample_combine(q_rows, km, k_rows, v_rows, m_blk, l_blk, acc, q_blk):
    n_seq, rows, head_dim = q_rows.shape
    n_blk, n_heads = km.shape[1], km.shape[2]
    seq_spec = pl.BlockSpec((None, rows, head_dim), lambda b: (b, 0, 0))
    stat_spec = pl.BlockSpec((None, rows, m_blk.shape[2]), lambda b: (b, 0, 0))
    part_spec = pl.BlockSpec((None, n_blk, rows, head_dim), lambda b: (b, 0, 0, 0))
    return pl.pallas_call(
        functools.partial(_sample_combine_kernel, n_heads=n_heads, q_blk=q_blk, scale=head_dim ** -0.5),
        out_shape=jax.ShapeDtypeStruct((n_seq, rows, head_dim), BF16),
        grid=(n_seq,),
        in_specs=[
            seq_spec,
            pl.BlockSpec((None, n_blk * n_heads, head_dim), lambda b: (b, 0, 0)),
            seq_spec, seq_spec, stat_spec, stat_spec, part_spec,
        ],
        out_specs=seq_spec,
        scratch_shapes=[pltpu.VMEM(((n_blk + 1) * n_heads, head_dim), F32)],
        compiler_params=_params("parallel"),
        name="sample_select_combine",
    )(q_rows, km.reshape(n_seq, n_blk * n_heads, head_dim), k_rows, v_rows, m_blk, l_blk, acc)


def _proj_ln_kernel(x_ref, w_ref, r_ref, g_ref, b_ref, y_ref, yb_ref, *, alpha):
    tm = x_ref.shape[0]
    n_parts = 2 if tm % (2 * BF16_SUBLANES) == 0 else 1
    for part in range(n_parts):
        rows = slice(part * tm // n_parts, (part + 1) * tm // n_parts)
        z = alpha * r_ref[rows, :] + _dot(x_ref[rows, :], w_ref[...])
        y = _layer_norm(z, g_ref[...], b_ref[...])
        y_ref[rows, :] = y
        yb_ref[rows, :] = y.astype(BF16)


def _proj_ln(x_bf, w_bf, resid, g, b, alpha):
    m, d_in = x_bf.shape
    d = w_bf.shape[1]
    tm = _row_tile(m, 512)
    row = lambda width: pl.BlockSpec((tm, width), lambda i: (i, 0))
    vec = pl.BlockSpec((1, d), lambda i: (0, 0))
    return pl.pallas_call(
        functools.partial(_proj_ln_kernel, alpha=alpha),
        out_shape=(jax.ShapeDtypeStruct((m, d), F32), jax.ShapeDtypeStruct((m, d), BF16)),
        grid=(m // tm,),
        in_specs=[row(d_in), pl.BlockSpec((d_in, d), lambda i: (0, 0)), row(d), vec, vec],
        out_specs=(row(d), row(d)),
        compiler_params=_params("parallel"),
        name="proj_residual_ln",
    )(x_bf, w_bf, resid, g.reshape(1, d), b.reshape(1, d))


def _ffn_ln_kernel(x_ref, wg_ref, wu_ref, wo_ref, r_ref, g_ref, b_ref, y_ref, yb_ref, *rest, alpha):
    acc_ref = rest[-1]
    f = pl.program_id(1)

    @pl.when(f == 0)
    def _():
        acc_ref[...] = jnp.zeros_like(acc_ref)

    wg, wu, wo = wg_ref[...].astype(BF16), wu_ref[...].astype(BF16), wo_ref[...].astype(BF16)
    for w, out_ref in zip((wg, wu, wo), rest[:-1]):
        out_ref[...] = w
    x = x_ref[...]
    gate = _dot(x, wg)
    up = _dot(x, wu)
    hidden = (gate * _sigmoid(gate) * up).astype(BF16)
    acc_ref[...] += _dot(hidden, wo)

    @pl.when(f == pl.num_programs(1) - 1)
    def _():
        y = _layer_norm(alpha * r_ref[...] + acc_ref[...], g_ref[...], b_ref[...])
        y_ref[...] = y
        yb_ref[...] = y.astype(BF16)


FFN_TILE = 512


def _ffn_ln(x_bf, wg_bf, wu_bf, wo_bf, resid, g, b, alpha):
    m, d = x_bf.shape
    d_ff = wo_bf.shape[0]
    tm = _row_tile(m, 512)
    tf = FFN_TILE
    assert d_ff % tf == 0
    row = lambda: pl.BlockSpec((tm, d), lambda i, f: (i, 0))
    vec = pl.BlockSpec((1, d), lambda i, f: (0, 0))
    w_in = pl.BlockSpec((d, tf), lambda i, f: (0, f))
    return pl.pallas_call(
        functools.partial(_ffn_ln_kernel, alpha=alpha),
        out_shape=(jax.ShapeDtypeStruct((m, d), F32), jax.ShapeDtypeStruct((m, d), BF16)),
        grid=(m // tm, d_ff // tf),
        in_specs=[row(), w_in, w_in, pl.BlockSpec((tf, d), lambda i, f: (f, 0)), row(), vec, vec],
        out_specs=(row(), row()),
        scratch_shapes=[pltpu.VMEM((tm, d), F32)],
        compiler_params=_params("parallel", "arbitrary"),
        name="swiglu_residual_ln",
    )(x_bf, wg_bf, wu_bf, wo_bf, resid, g.reshape(1, d), b.reshape(1, d))


def _ffn_ln_casting(x_bf, w_in, w_out, layer, resid, g, b, alpha):
    m, d = x_bf.shape
    d_ff = w_out.shape[1]
    tf = FFN_TILE
    assert d_ff % tf == 0
    nf = d_ff // tf
    row = lambda: pl.BlockSpec((m, d), lambda i, f: (0, 0))
    vec = pl.BlockSpec((1, d), lambda i, f: (0, 0))
    w_in_bf = jax.ShapeDtypeStruct((d, d_ff), BF16)
    w_in_tile = pl.BlockSpec((d, tf), lambda i, f: (0, f))
    w_out_tile = pl.BlockSpec((tf, d), lambda i, f: (f, 0))
    return pl.pallas_call(
        functools.partial(_ffn_ln_kernel, alpha=alpha),
        out_shape=(jax.ShapeDtypeStruct((m, d), F32), jax.ShapeDtypeStruct((m, d), BF16), w_in_bf, w_in_bf,
                   jax.ShapeDtypeStruct((d_ff, d), BF16)),
        grid=(1, nf),
        in_specs=[
            row(),
            pl.BlockSpec((None, d, tf), lambda i, f: (layer, 0, f)),
            pl.BlockSpec((None, d, tf), lambda i, f: (layer, 0, nf + f)),
            pl.BlockSpec((None, tf, d), lambda i, f: (layer, f, 0)),
            row(), vec, vec,
        ],
        out_specs=(row(), row(), w_in_tile, w_in_tile, w_out_tile),
        scratch_shapes=[pltpu.VMEM((m, d), F32)],
        compiler_params=_params("arbitrary", "arbitrary"),
        name="swiglu_residual_ln_cast",
    )(x_bf, w_in, w_in, w_out, resid, g.reshape(1, d), b.reshape(1, d))


def _matmul_kernel(x_ref, w_ref, o_ref):
    o_ref[...] = _dot(x_ref[...], w_ref[...])


def _matmul(x_bf, w_bf):
    m, d = x_bf.shape
    n_out = w_bf.shape[1]
    tm = _row_tile(m, 1024)
    tn = 1024
    return pl.pallas_call(
        _matmul_kernel,
        out_shape=jax.ShapeDtypeStruct((m, n_out), F32),
        grid=(n_out // tn, m // tm),
        in_specs=[pl.BlockSpec((tm, d), lambda j, i: (i, 0)), pl.BlockSpec((d, tn), lambda j, i: (0, j))],
        out_specs=pl.BlockSpec((tm, tn), lambda j, i: (i, j)),
        compiler_params=_params("parallel", "parallel"),
        name="hgrn_in_proj",
    )(x_bf, w_bf)


def _hgrn_level_matrix(chunk):
    n_lev = int(math.log2(chunk))
    assert 2 ** n_lev == chunk
    t = np.arange(chunk)[:, None]
    s = np.arange(chunk)[None, :]
    mats = [s <= t]
    for lev in range(1, n_lev + 1):
        size = 2 ** lev
        mats.append(s < (t // size) * size + size // 2)
    return np.concatenate(mats, 0).astype(np.float32), n_lev


def _hgrn_kernel(q_ref, f_ref, i_ref, g_ref, lbl_ref, ng_ref, cm_ref, s0_ref, o_ref, s_ref, st_ref,
                 *, layer, n_valid, n_lev, n_heads, head_group):
    c = pl.program_id(1)
    chunk, d = q_ref.shape
    dk = d // n_heads

    @pl.when(c == 0)
    def _():
        for h in range(n_heads):
            st_ref[h] = s0_ref[h].T

    logits = lbl_ref[...]
    lrow = lax.broadcasted_iota(jnp.int32, logits.shape, 0)
    e = jnp.exp(logits - jnp.max(logits, axis=0, keepdims=True))
    sm = e / jnp.sum(e, axis=0, keepdims=True)
    lb = jnp.sum(jnp.where((lrow >= 1) & (lrow <= layer), sm, 0.0), axis=0, keepdims=True)

    row = lax.broadcasted_iota(jnp.int32, (chunk, d), 0)
    qraw = q_ref[...]
    q = qraw * _sigmoid(qraw)
    f = lb + (1.0 - lb) * _sigmoid(f_ref[...])
    k = 1.0 - f
    g = jnp.log(f)
    if n_valid < chunk:
        live = row < n_valid
        k = jnp.where(live, k, 0.0)
        g = jnp.where(live, g, 0.0)
    v = i_ref[...]
    vb = v.astype(BF16)
    graw = g_ref[...]
    out_gate = graw * _sigmoid(graw)

    cm = cm_ref[...]
    g1 = g.astype(BF16)
    r1 = g - g1.astype(F32)
    g2 = r1.astype(BF16)
    g3 = (r1 - g2.astype(F32)).astype(BF16)
    b_all = _dot(cm, jnp.concatenate([g1, g2, g3], axis=0))
    b = b_all[0:chunk]

    t_id = lax.broadcasted_iota(jnp.int32, (chunk, chunk), 0)
    s_id = lax.broadcasted_iota(jnp.int32, (chunk, chunk), 1)
    row_h = lax.broadcasted_iota(jnp.int32, (chunk, dk), 0)
    qb = q.astype(BF16)
    kb = k.astype(BF16)
    decays, rights, pairs = [], [], []
    for lev in range(1, n_lev + 1):
        size = 2 ** lev
        half = size // 2
        decays.append(jnp.exp(-jnp.abs(b - b_all[lev * chunk:(lev + 1) * chunk])))
        rights.append((row_h % size) >= half)
        pairs.append(((t_id // size) == (s_id // size)) & ((t_id % size) >= half) & ((s_id % size) < half))

    q_decayed = (q * jnp.exp(b)).astype(BF16)
    b_last = b[chunk - 1:chunk, :]
    k_decayed = (k * jnp.exp(b_last - b)).astype(BF16)
    state_decay = jnp.exp(b_last)
    ng = ng_ref[...]

    def issue_matmuls(h):
        hs = slice(h * dk, (h + 1) * dk)
        st = st_ref[h]
        qh, kh = q[:, hs], k[:, hs]
        factors = [(jnp.where(right, qh, kh) * decay[:, hs]).astype(BF16) for right, decay in zip(rights, decays)]
        return (hs, st, _dot_nt(qb[:, hs], kb[:, hs]), [_dot_nt(x, x) for x in factors],
                _dot_nt(q_decayed[:, hs], st.astype(BF16)), _dot(v[:, hs].T.astype(BF16), k_decayed[:, hs]))

    def finish(h, issued):
        hs, st, diag, level, inter, update = issued
        a = jnp.where(t_id == s_id, diag, 0.0)
        for prod, pair in zip(level, pairs):
            a = a + jnp.where(pair, prod, 0.0)
        o = inter + _dot(a.astype(BF16), vb[:, hs])
        st_ref[h] = st * state_decay[:, hs] + update
        o = o * lax.rsqrt(jnp.mean(o * o, axis=-1, keepdims=True) + RMS_EPS) * ng
        o_ref[:, hs] = (o * out_gate[:, hs]).astype(o_ref.dtype)

    groups = [range(g, min(g + head_group, n_heads)) for g in range(0, n_heads, head_group)]
    issued = {h: issue_matmuls(h) for h in groups[0]}
    for gi, grp in enumerate(groups):
        if gi + 1 < len(groups):
            issued.update({h: issue_matmuls(h) for h in groups[gi + 1]})
        for h in grp:
            finish(h, issued.pop(h))

    @pl.when(c == pl.num_programs(1) - 1)
    def _():
        for h in range(n_heads):
            s_ref[h] = st_ref[h].T


def _hgrn_recurrence(y, s0, lb_logits, norm_g, layer, n_valid, chunk, head_group=4):
    batch, lp, d4 = y.shape
    d = d4 // 4
    _, n_heads, dk, dv = s0.shape
    assert n_heads * dk == d and dk == dv
    depth = lb_logits.shape[0]
    cm_np, n_lev = _hgrn_level_matrix(chunk)
    cm = jnp.asarray(np.concatenate([cm_np] * 3, axis=1), dtype=BF16)

    def quarter(qt):
        return pl.BlockSpec((None, chunk, d), lambda b, c: (b, c, qt))

    state_spec = pl.BlockSpec((None, n_heads, dk, dv), lambda b, c: (b, 0, 0, 0))
    return pl.pallas_call(
        functools.partial(_hgrn_kernel, layer=layer, n_valid=n_valid, n_lev=n_lev, n_heads=n_heads,
                          head_group=head_group),
        out_shape=(jax.ShapeDtypeStruct((batch, lp, d), BF16),
                   jax.ShapeDtypeStruct((batch, n_heads, dk, dv), F32)),
        grid=(batch, lp // chunk),
        in_specs=[
            quarter(0), quarter(1), quarter(2), quarter(3),
            pl.BlockSpec((depth, d), lambda b, c: (0, 0)),
            pl.BlockSpec((1, dv), lambda b, c: (0, 0)),
            pl.BlockSpec(cm.shape, lambda b, c: (0, 0)),
            state_spec,
        ],
        out_specs=(pl.BlockSpec((None, chunk, d), lambda b, c: (b, c, 0)), state_spec),
        scratch_shapes=[pltpu.VMEM((n_heads, dv, dk), F32)],
        compiler_params=_params("parallel", "arbitrary"),
        name="hgrn_recurrence",
    )(y, y, y, y, lb_logits, norm_g.reshape(1, dv), cm, s0)


def _moba_layer(xp, xp_bf, xs, xs_bf, batch, n_seq, cache_k, cache_v, layer, page_table, w_qkv_bf, w_o_bf,
                ln_g, ln_b, alpha):
    d = xp.shape[-1]
    head_dim = d // N_HEADS
    seq = xp.shape[0] // batch
    n_new = xs.shape[0] // n_seq
    past = page_table.shape[1] * cache_k.shape[2]
    assert past % MOBA_BLOCK == 0 and n_new <= MOBA_BLOCK and cache_k.shape[3] == N_HEADS

    cos_p, sin_p = _rope_tables(jnp.arange(seq, dtype=jnp.int32), head_dim)
    qp, kp, vp = _qkv_rope(xp_bf, w_qkv_bf, jnp.tile(cos_p, (batch, 1)), jnp.tile(sin_p, (batch, 1)))
    op = _moba_prompt_attention(qp, kp, vp, batch)
    yp, yp_bf = _proj_ln(op, w_o_bf, xp, ln_g, ln_b, alpha)

    cos_s, sin_s = _rope_tables(past + jnp.arange(n_new, dtype=jnp.int32), head_dim)
    qs, ks, vs = _qkv_rope(xs_bf, w_qkv_bf, jnp.tile(cos_s, (n_seq, 1)), jnp.tile(sin_s, (n_seq, 1)))
    rows = lambda a: a.reshape(n_seq, n_new * N_HEADS, head_dim)
    km, m_blk, l_blk, acc = _sample_stream(rows(qs), cache_k, cache_v, layer, page_table)
    os_ = _sample_combine(rows(qs), km, rows(ks), rows(vs), m_blk, l_blk, acc, past // MOBA_BLOCK)
    ys, ys_bf = _proj_ln(os_.reshape(n_seq * n_new, d), w_o_bf, xs, ln_g, ln_b, alpha)

    kv_shape = lambda n, l: (n, l, N_HEADS, head_dim)
    return (yp, yp_bf, ys, ys_bf, kp.reshape(kv_shape(batch, seq)), vp.reshape(kv_shape(batch, seq)),
            ks.reshape(kv_shape(n_seq, n_new)), vs.reshape(kv_shape(n_seq, n_new)))


def _hgrn_group(x, x_bf, n_seq, s0, w_in_bf, w_o_bf, lb_logits, norm_g, layer, ln_g, ln_b, alpha):
    m, d = x.shape
    n_heads = d // HGRN_EXPAND
    seq = m // n_seq
    y = _matmul(x_bf, w_in_bf).reshape(n_seq, seq, 4 * d)
    if seq % HGRN_CHUNK == 0:
        chunk, lp = HGRN_CHUNK, seq
    else:
        chunk = lp = max(BF16_SUBLANES, int(2 ** math.ceil(math.log2(seq))))
        y = jnp.pad(y, ((0, 0), (0, lp - seq), (0, 0)))
    o, s = _hgrn_recurrence(y, s0, lb_logits, norm_g, layer, min(seq, chunk), chunk)
    o = o[:, :seq].reshape(m, d)
    y_out, y_out_bf = _proj_ln(o, w_o_bf, x, ln_g, ln_b, alpha)
    return y_out, y_out_bf, s


def kernel(x_prompt, x_sample, cache_k, cache_v, state_hgrn, page_table, attn_w_qkv, attn_w_o, hgrn_w_in,
           hgrn_lb_logits, hgrn_norm_g, hgrn_w_o, ffn_w_in, ffn_w_out, ln_mix_g, ln_mix_b, ln_ffn_g, ln_ffn_b):
    batch, seq, d = x_prompt.shape
    n_seq, n_new, _ = x_sample.shape
    depth = ffn_w_in.shape[0]
    alpha = (2.0 * depth) ** 0.25
    n_hgrn_heads = d // HGRN_EXPAND

    yp = x_prompt.reshape(batch * seq, d)
    ys = x_sample.reshape(n_seq * n_new, d)
    yp_bf, ys_bf = yp.astype(BF16), ys.astype(BF16)
    pk, pv, ps, sk, sv, ss = [], [], [], [], [], []
    for i in range(depth):
        if i % 2 == 0:
            a = i // 2
            yp, yp_bf, ys, ys_bf, kp, vp, kn, vn = _moba_layer(
                yp, yp_bf, ys, ys_bf, batch, n_seq, cache_k, cache_v, a, page_table,
                _layer_bf16(attn_w_qkv, a), _layer_bf16(attn_w_o, a), ln_mix_g[i], ln_mix_b[i], alpha)
            pk.append(kp)
            pv.append(vp)
            sk.append(kn)
            sv.append(vn)
        else:
            r = i // 2
            w_in_bf, w_o_bf = _layer_bf16(hgrn_w_in, r), _layer_bf16(hgrn_w_o, r)
            s0p = jnp.zeros((batch, n_hgrn_heads, HGRN_EXPAND, d // n_hgrn_heads), state_hgrn.dtype)
            yp, yp_bf, sp = _hgrn_group(yp, yp_bf, batch, s0p, w_in_bf, w_o_bf, hgrn_lb_logits, hgrn_norm_g[r],
                                        i, ln_mix_g[i], ln_mix_b[i], alpha)
            ys, ys_bf, sn = _hgrn_group(ys, ys_bf, n_seq, state_hgrn[r], w_in_bf, w_o_bf, hgrn_lb_logits,
                                        hgrn_norm_g[r], i, ln_mix_g[i], ln_mix_b[i], alpha)
            ps.append(sp)
            ss.append(sn)
        ys, ys_bf, wg_bf, wu_bf, wo_bf = _ffn_ln_casting(ys_bf, ffn_w_in, ffn_w_out, i, ys, ln_ffn_g[i],
                                                         ln_ffn_b[i], alpha)
        yp, yp_bf = _ffn_ln(yp_bf, wg_bf, wu_bf, wo_bf, yp, ln_ffn_g[i], ln_ffn_b[i], alpha)
    return (yp.reshape(batch, seq, d), ys.reshape(n_seq, n_new, d), jnp.stack(pk), jnp.stack(pv), jnp.stack(ps),
            jnp.stack(sk), jnp.stack(sv), jnp.stack(ss))
```
